```python
import jax, jax.numpy as jnp
from jax import lax
import numpy as np

D_MODEL = 1024
BATCH = 8
SEQ = 2048
DEPTH = 1

CHUNK = 64
Q_BLOCK = 128
HEAD_DIM = 64
N_HEADS_SB = 8
N_HEADS_FOX = 8
WIDTH_SB = N_HEADS_SB * HEAD_DIM
WIDTH_FOX = N_HEADS_FOX * HEAD_DIM
IN_COLS = 3 * WIDTH_SB + 3 * WIDTH_FOX + N_HEADS_FOX
D_FF = 2816
RMS_EPS = 1e-6
ATTN_SCALE = HEAD_DIM ** -0.5
FORGET_BIAS_MEAN = 2.0

kernel_name = "macaron_stickbreak_forgetting_gated_block"


def rms_norm(x, g):
    xf = x.astype(jnp.float32)
    y = xf * lax.rsqrt(jnp.mean(xf * xf, axis=-1, keepdims=True) + RMS_EPS)
    return (y * g.astype(jnp.float32)).astype(x.dtype)


def swiglu(h, w_gate, w_up, w_down):
    return (jax.nn.silu(h @ w_gate) * (h @ w_up)) @ w_down


def split_heads(t, n_heads):
    b, s, _ = t.shape
    return t.reshape(b, s, n_heads, HEAD_DIM).transpose(0, 2, 1, 3)


def merge_heads(t):
    b, h, s, d = t.shape
    return t.transpose(0, 2, 1, 3).reshape(b, s, h * d)


def stick_breaking_attention(q, k, v):
    seq = q.shape[2]
    outs = []
    for i in range(seq // Q_BLOCK):
        q0 = i * Q_BLOCK
        k_end = q0 + Q_BLOCK
        z = jnp.einsum('bhqd,bhkd->bhqk', q[:, :, q0:k_end], k[:, :, :k_end]).astype(jnp.float32) * ATTN_SCALE
        t_pos = q0 + jnp.arange(Q_BLOCK)[:, None]
        s_pos = jnp.arange(k_end)[None, :]
        strict = s_pos < t_pos
        log_not_beta = jnp.where(strict, jax.nn.log_sigmoid(-z), 0.0)
        between = lax.cumsum(log_not_beta, axis=3, reverse=True) - log_not_beta
        weights = jnp.where(strict, jnp.exp(jax.nn.log_sigmoid(z) + between), 0.0)
        outs.append(jnp.einsum('bhqk,bhkd->bhqd', weights.astype(v.dtype), v[:, :, :k_end]))
    return jnp.concatenate(outs, axis=2)


def forgetting_attention(q, k, v, log_f_cum):
    seq = q.shape[2]
    outs = []
    for i in range(seq // Q_BLOCK):
        q0 = i * Q_BLOCK
        k_end = q0 + Q_BLOCK
        logits = jnp.einsum('bhqd,bhkd->bhqk', q[:, :, q0:k_end], k[:, :, :k_end]).astype(jnp.float32) * ATTN_SCALE
        logits = logits + log_f_cum[:, :, q0:k_end, None] - log_f_cum[:, :, None, :k_end]
        t_pos = q0 + jnp.arange(Q_BLOCK)[:, None]
        s_pos = jnp.arange(k_end)[None, :]
        logits = jnp.where(s_pos <= t_pos, logits, -jnp.inf)
        probs = jax.nn.softmax(logits, axis=-1)
        outs.append(jnp.einsum('bhqk,bhkd->bhqd', probs.astype(v.dtype), v[:, :, :k_end]))
    return jnp.concatenate(outs, axis=2)


def setup_inputs(seed: int = 0) -> dict:
    key = jax.random.key(seed)
    ks = jax.random.split(key, 20)
    f32 = jnp.float32

    def dense(k, fan_in, fan_out):
        return jax.random.normal(k, (DEPTH, fan_in, fan_out), f32) * fan_in ** -0.5

    def gain(k, shape):
        return 1.0 + 0.02 * jax.random.normal(k, shape, f32)

    return {
        'x': jax.random.normal(ks[0], (BATCH, SEQ, D_MODEL), f32),
        'norm_ffn1': gain(ks[1], (DEPTH, D_MODEL)),
        'w_ffn1_gate': dense(ks[2], D_MODEL, D_FF),
        'w_ffn1_up': dense(ks[3], D_MODEL, D_FF),
        'w_ffn1_down': dense(ks[4], D_FF, D_MODEL),
        'norm_mix': gain(ks[5], (DEPTH, D_MODEL)),
        'w_in': dense(ks[6], D_MODEL, IN_COLS),
        'b_forget': FORGET_BIAS_MEAN + 0.1 * jax.random.normal(ks[7], (DEPTH, N_HEADS_FOX), f32),
        'w_gate': dense(ks[8], D_MODEL, 2 * D_MODEL),
        'b_gate': 0.02 * jax.random.normal(ks[9], (DEPTH, 2 * D_MODEL), f32),
        'w_up_a': dense(ks[10], WIDTH_SB, D_MODEL),
        'w_up_b': dense(ks[11], WIDTH_FOX, D_MODEL),
        'w_out': dense(ks[12], D_MODEL, D_MODEL),
        'norm_ffn2': gain(ks[13], (DEPTH, D_MODEL)),
        'w_ffn2_gate': dense(ks[14], D_MODEL, D_FF),
        'w_ffn2_up': dense(ks[15], D_MODEL, D_FF),
        'w_ffn2_down': dense(ks[16], D_FF, D_MODEL),
        'norm_final': gain(ks[17], (D_MODEL,)),
    }


def reference(x, norm_ffn1, w_ffn1_gate, w_ffn1_up, w_ffn1_down, norm_mix, w_in, b_forget,
              w_gate, b_gate, w_up_a, w_up_b, w_out, norm_ffn2, w_ffn2_gate, w_ffn2_up,
              w_ffn2_down, norm_final):
    splits = np.cumsum([WIDTH_SB, WIDTH_SB, WIDTH_SB, WIDTH_FOX, WIDTH_FOX, WIDTH_FOX]).tolist()
    for l in range(DEPTH):
        x = x + 0.5 * swiglu(rms_norm(x, norm_ffn1[l]), w_ffn1_gate[l], w_ffn1_up[l], w_ffn1_down[l])

        h = rms_norm(x, norm_mix[l])
        proj = h @ w_in[l]
        q_a, k_a, v_a, q_b, k_b, v_b, f_logit = jnp.split(proj, splits, axis=-1)

        y_a = merge_heads(stick_breaking_attention(
            split_heads(q_a, N_HEADS_SB), split_heads(k_a, N_HEADS_SB), split_heads(v_a, N_HEADS_SB)))

        log_f = jax.nn.log_sigmoid((f_logit + b_forget[l]).astype(jnp.float32))
        log_f_cum = jnp.cumsum(log_f, axis=1).transpose(0, 2, 1)
        y_b = merge_heads(forgetting_attention(
            split_heads(q_b, N_HEADS_FOX), split_heads(k_b, N_HEADS_FOX), split_heads(v_b, N_HEADS_FOX),
            log_f_cum))

        gates = jax.nn.sigmoid(h @ w_gate[l] + b_gate[l])
        g_a, g_b = jnp.split(gates, 2, axis=-1)
        mixed = g_a * (y_a @ w_up_a[l]) + g_b * (y_b @ w_up_b[l])
        x = x + mixed @ w_out[l]

        x = x + 0.5 * swiglu(rms_norm(x, norm_ffn2[l]), w_ffn2_gate[l], w_ffn2_up[l], w_ffn2_down[l])
    return rms_norm(x, norm_final)
```

```python
import functools

import jax
import jax.numpy as jnp
from jax import lax
from jax.experimental import pallas as pl
from jax.experimental.pallas import tpu as pltpu

D_MODEL = 1024
D_FF = 2816
HEAD_DIM = 64
N_HEADS = 8
WIDTH = N_HEADS * HEAD_DIM
RMS_EPS = 1e-6
ATTN_SCALE = HEAD_DIM ** -0.5

LANES = 128
HEADS_PER_BLOCK = LANES // HEAD_DIM
N_HEAD_BLOCKS = WIDTH // LANES

TM_FFN = 512
TF = 256
TM_PROJ = 512
TQ = 256
TK = 256
VMEM_LIMIT = 56 * 1024 * 1024

F32 = jnp.float32
BF16 = jnp.bfloat16
NEG_BIG = -1e30


def _rms(x, g):
    ms = jnp.mean(x * x, axis=-1, keepdims=True)
    return x * lax.rsqrt(ms + RMS_EPS) * g


def _const_spec(shape):
    nd = len(shape)
    return pl.BlockSpec(shape, lambda *_: (0,) * nd, pipeline_mode=pl.Buffered(1))


def _ffn_body(x_ref, g_ref, wg_ref, wu_ref, wd_ref, gf_ref, o_ref, *, final_norm):
    x = x_ref[...]
    xn = _rms(x, g_ref[...]).astype(BF16)
    acc = jnp.zeros(x.shape, F32)
    for c in range(D_FF // TF):
        g = jnp.dot(xn, wg_ref[c], preferred_element_type=F32)
        u = jnp.dot(xn, wu_ref[c], preferred_element_type=F32)
        h = (g * jax.nn.sigmoid(g) * u).astype(BF16)
        acc = acc + jnp.dot(h, wd_ref[c], preferred_element_type=F32)
    y = x + 0.5 * acc
    if final_norm:
        y = _rms(y, gf_ref[...])
    o_ref[...] = y


def _ffn(x2d, g, wg, wu, wd, gf, *, final_norm):
    t = x2d.shape[0]
    nc = D_FF // TF
    tile = pl.BlockSpec((TM_FFN, D_MODEL), lambda i: (i, 0))
    return pl.pallas_call(
        functools.partial(_ffn_body, final_norm=final_norm),
        grid=(t // TM_FFN,),
        in_specs=[tile, _const_spec((1, D_MODEL)), _const_spec((nc, D_MODEL, TF)),
                  _const_spec((nc, D_MODEL, TF)), _const_spec((nc, TF, D_MODEL)),
                  _const_spec((1, D_MODEL))],
        out_specs=tile,
        out_shape=jax.ShapeDtypeStruct((t, D_MODEL), F32),
        compiler_params=pltpu.CompilerParams(dimension_semantics=("parallel",),
                                             vmem_limit_bytes=VMEM_LIMIT),
        name="ffn_final" if final_norm else "ffn",
    )(x2d, g, wg, wu, wd, gf)


def _proj_body(x_ref, g_ref, w_ref, bf_ref, tri_ref, qkv_ref, cfc_ref, cfr_ref, carry_ref):
    @pl.when(pl.program_id(1) == 0)
    def _():
        carry_ref[...] = jnp.zeros_like(carry_ref)

    h = _rms(x_ref[...], g_ref[...]).astype(BF16)
    for c in range(6):
        sl = slice(c * WIDTH, (c + 1) * WIDTH)
        p = jnp.dot(h, w_ref[:, sl], preferred_element_type=F32)
        if c in (0, 3):
            p = p * ATTN_SCALE
        qkv_ref[:, sl] = p.astype(BF16)

    fl = jnp.dot(h, w_ref[:, 6 * WIDTH:], preferred_element_type=F32) + bf_ref[...]
    lf = jnp.minimum(fl, 0.0) - jnp.log1p(jnp.exp(-jnp.abs(fl)))
    hi = lf.astype(BF16)
    r1 = lf - hi.astype(F32)
    mid = r1.astype(BF16)
    lo = (r1 - mid.astype(F32)).astype(BF16)
    cs = jnp.dot(tri_ref[...], jnp.concatenate([hi, mid, lo], axis=1), preferred_element_type=F32)
    cf = cs[:, :LANES] + cs[:, LANES:2 * LANES] + cs[:, 2 * LANES:] + carry_ref[...]
    carry_ref[...] = cf[TM_PROJ - 1:TM_PROJ, :]
    cfc_ref[...] = cf
    cfr_ref[0] = cf.T[:N_HEADS, :]


def _proj(x2d, g, w, bf, tri, batch, seq):
    t = x2d.shape[0]
    ns = seq // TM_PROJ
    ncol = w.shape[1]
    return pl.pallas_call(
        _proj_body,
        grid=(batch, ns),
        in_specs=[pl.BlockSpec((TM_PROJ, D_MODEL), lambda b, s: (b * ns + s, 0)),
                  _const_spec((1, D_MODEL)), _const_spec((D_MODEL, ncol)),
                  _const_spec((1, LANES)), _const_spec((TM_PROJ, TM_PROJ))],
        out_specs=[pl.BlockSpec((TM_PROJ, 6 * WIDTH), lambda b, s: (b * ns + s, 0)),
                   pl.BlockSpec((TM_PROJ, LANES), lambda b, s: (b * ns + s, 0)),
                   pl.BlockSpec((1, N_HEADS, TM_PROJ), lambda b, s: (b, 0, s))],
        out_shape=[jax.ShapeDtypeStruct((t, 6 * WIDTH), BF16),
                   jax.ShapeDtypeStruct((t, LANES), F32),
                   jax.ShapeDtypeStruct((batch, N_HEADS, seq), F32)],
        scratch_shapes=[pltpu.VMEM((1, LANES), F32)],
        compiler_params=pltpu.CompilerParams(dimension_semantics=("arbitrary", "arbitrary"),
                                             vmem_limit_bytes=VMEM_LIMIT),
        name="proj",
    )(x2d, g, w, bf, tri)


def _head_split(q):
    lane = lax.broadcasted_iota(jnp.int32, q.shape, 1)
    zero = jnp.zeros_like(q)
    return [jnp.where(lane < HEAD_DIM, q, zero), jnp.where(lane >= HEAD_DIM, q, zero)]


def _qk(qh, kj):
    return lax.dot_general(qh, kj, (((1,), (1,)), ((), ())), preferred_element_type=F32)


def _attn_specs(batch, seq, col_q, col_k, col_v):
    nq = seq // TQ
    q_spec = pl.BlockSpec((TQ, LANES), lambda b, hp, i: (b * nq + i, col_q + hp))
    k_spec = pl.BlockSpec((seq, LANES), lambda b, hp, i: (b, col_k + hp))
    v_spec = pl.BlockSpec((seq, LANES), lambda b, hp, i: (b, col_v + hp))
    o_spec = pl.BlockSpec((TQ, LANES), lambda b, hp, i: (b * nq + i, hp))
    return nq, q_spec, k_spec, v_spec, o_spec


def _sb_body(q_ref, k_ref, v_ref, u_ref, o_ref, acc_ref):
    i = pl.program_id(2)
    qh = _head_split(q_ref[...])
    t_idx = lax.broadcasted_iota(jnp.int32, (TQ, TK), 0)
    s_idx = lax.broadcasted_iota(jnp.int32, (TQ, TK), 1)
    strict = s_idx < t_idx

    def block(j, h, r, diag):
        kj = k_ref[pl.ds(pl.multiple_of(j * TK, TK), TK), :]
        vj = v_ref[pl.ds(pl.multiple_of(j * TK, TK), TK), :]
        z = _qk(qh[h], kj)
        sp = jnp.maximum(z, 0.0) + jnp.log1p(jnp.exp(-jnp.abs(z)))
        if diag:
            sp = jnp.where(strict, sp, 0.0)
        hi = sp.astype(BF16)
        lo = (sp - hi.astype(F32)).astype(BF16)
        c = jnp.dot(jnp.concatenate([hi, lo], axis=1), u_ref[...], preferred_element_type=F32) + r
        w = jnp.exp(z - c)
        if diag:
            w = jnp.where(strict, w, 0.0)
        acc_ref[h] += jnp.dot(w.astype(BF16), vj, preferred_element_type=F32)
        return c[:, 0:1]

    acc_ref[...] = jnp.zeros_like(acc_ref)
    zero_r = jnp.zeros((TQ, 1), F32)
    r0 = block(i, 0, zero_r, True)
    r1 = block(i, 1, zero_r, True)

    def step(jj, carry):
        j = i - 1 - jj
        return block(j, 0, carry[0], False), block(j, 1, carry[1], False)

    lax.fori_loop(0, i, step, (r0, r1))
    lane = lax.broadcasted_iota(jnp.int32, (TQ, LANES), 1)
    o_ref[...] = jnp.where(lane < HEAD_DIM, acc_ref[0], acc_ref[1]).astype(BF16)


def _sb_attn(qkv, tri_u, batch, seq):
    nq, q_spec, k_spec, v_spec, o_spec = _attn_specs(batch, seq, 0, N_HEAD_BLOCKS, 2 * N_HEAD_BLOCKS)
    return pl.pallas_call(
        _sb_body,
        grid=(batch, N_HEAD_BLOCKS, nq),
        in_specs=[q_spec, k_spec, v_spec, _const_spec((2 * TK, TK))],
        out_specs=o_spec,
        out_shape=jax.ShapeDtypeStruct((batch * seq, WIDTH), BF16),
        scratch_shapes=[pltpu.VMEM((HEADS_PER_BLOCK, TQ, LANES), F32)],
        compiler_params=pltpu.CompilerParams(dimension_semantics=("parallel", "parallel", "arbitrary"),
                                             vmem_limit_bytes=VMEM_LIMIT),
        name="sb_attn",
    )(qkv, qkv, qkv, tri_u)


def _fox_body(q_ref, k_ref, v_ref, cfc_ref, cfr_ref, o_ref, acc_ref):
    hp = pl.program_id(1)
    i = pl.program_id(2)
    qh = _head_split(q_ref[...])
    t_idx = lax.broadcasted_iota(jnp.int32, (TQ, TK), 0)
    s_idx = lax.broadcasted_iota(jnp.int32, (TQ, TK), 1)
    causal = s_idx <= t_idx
    lane = lax.broadcasted_iota(jnp.int32, (TQ, LANES), 1)
    cfc = cfc_ref[...]
    heads = [hp * HEADS_PER_BLOCK + h for h in range(HEADS_PER_BLOCK)]
    cf_t = [jnp.sum(jnp.where(lane == hd, cfc, 0.0), axis=1, keepdims=True) for hd in heads]

    def logits(j, h):
        off = pl.multiple_of(j * TK, TK)
        z = _qk(qh[h], k_ref[pl.ds(off, TK), :])
        cf_s = cfr_ref[0, pl.ds(heads[h], 1), pl.ds(off, TK)]
        return z + cf_t[h] - cf_s

    def pv(p, j):
        return jnp.dot(p.astype(BF16), v_ref[pl.ds(pl.multiple_of(j * TK, TK), TK), :],
                       preferred_element_type=F32)

    state = []
    for h in range(HEADS_PER_BLOCK):
        s = jnp.where(causal, logits(i, h), NEG_BIG)
        m = jnp.max(s, axis=1, keepdims=True)
        p = jnp.exp(s - m)
        acc_ref[h] = pv(p, i)
        state += [m, jnp.sum(p, axis=1, keepdims=True)]

    def step(j, carry):
        out = []
        for h in range(HEADS_PER_BLOCK):
            m, l = carry[2 * h], carry[2 * h + 1]
            s = logits(j, h)
            m_new = jnp.maximum(m, jnp.max(s, axis=1, keepdims=True))
            alpha = jnp.exp(m - m_new)
            p = jnp.exp(s - m_new)
            acc_ref[h] = alpha * acc_ref[h] + pv(p, j)
            out += [m_new, alpha * l + jnp.sum(p, axis=1, keepdims=True)]
        return tuple(out)

    state = lax.fori_loop(0, i, step, tuple(state))
    y = [acc_ref[h] / state[2 * h + 1] for h in range(HEADS_PER_BLOCK)]
    o_ref[...] = jnp.where(lane < HEAD_DIM, y[0], y[1]).astype(BF16)


def _fox_attn(qkv, cf_col, cf_row, batch, seq):
    nq, q_spec, k_spec, v_spec, o_spec = _attn_specs(
        batch, seq, 3 * N_HEAD_BLOCKS, 4 * N_HEAD_BLOCKS, 5 * N_HEAD_BLOCKS)
    return pl.pallas_call(
        _fox_body,
        grid=(batch, N_HEAD_BLOCKS, nq),
        in_specs=[q_spec, k_spec, v_spec,
                  pl.BlockSpec((TQ, LANES), lambda b, hp, i: (b * nq + i, 0)),
                  pl.BlockSpec((1, N_HEADS, seq), lambda b, hp, i: (b, 0, 0))],
        out_specs=o_spec,
        out_shape=jax.ShapeDtypeStruct((batch * seq, WIDTH), BF16),
        scratch_shapes=[pltpu.VMEM((HEADS_PER_BLOCK, TQ, LANES), F32)],
        compiler_params=pltpu.CompilerParams(dimension_semantics=("parallel", "parallel", "arbitrary"),
                                             vmem_limit_bytes=VMEM_LIMIT),
        name="fox_attn",
    )(qkv, qkv, qkv, cf_col, cf_row)


def _post_body(x_ref, ya_ref, yb_ref, g_ref, wgate_ref, bgate_ref, wua_ref, wub_ref, wout_ref, o_ref):
    x = x_ref[...]
    h = _rms(x, g_ref[...]).astype(BF16)
    ya = ya_ref[...]
    yb = yb_ref[...]
    acc = jnp.zeros(x.shape, F32)
    nchunk = D_MODEL // WIDTH
    for c in range(nchunk):
        sl_a = slice(c * WIDTH, (c + 1) * WIDTH)
        sl_b = slice(D_MODEL + c * WIDTH, D_MODEL + (c + 1) * WIDTH)
        ga = jax.nn.sigmoid(jnp.dot(h, wgate_ref[:, sl_a], preferred_element_type=F32) + bgate_ref[:, sl_a])
        gb = jax.nn.sigmoid(jnp.dot(h, wgate_ref[:, sl_b], preferred_element_type=F32) + bgate_ref[:, sl_b])
        ua = jnp.dot(ya, wua_ref[:, sl_a], preferred_element_type=F32)
        ub = jnp.dot(yb, wub_ref[:, sl_a], preferred_element_type=F32)
        mixed = (ga * ua + gb * ub).astype(BF16)
        acc = acc + jnp.dot(mixed, wout_ref[sl_a, :], preferred_element_type=F32)
    o_ref[...] = x + acc


def _post(x2d, ya, yb, g, wgate, bgate, wua, wub, wout):
    t = x2d.shape[0]
    tile = pl.BlockSpec((TM_FFN, D_MODEL), lambda i: (i, 0))
    ytile = pl.BlockSpec((TM_FFN, WIDTH), lambda i: (i, 0))
    return pl.pallas_call(
        _post_body,
        grid=(t // TM_FFN,),
        in_specs=[tile, ytile, ytile, _const_spec((1, D_MODEL)), _const_spec((D_MODEL, 2 * D_MODEL)),
                  _const_spec((1, 2 * D_MODEL)), _const_spec((WIDTH, D_MODEL)),
                  _const_spec((WIDTH, D_MODEL)), _const_spec((D_MODEL, D_MODEL))],
        out_specs=tile,
        out_shape=jax.ShapeDtypeStruct((t, D_MODEL), F32),
        compiler_params=pltpu.CompilerParams(dimension_semantics=("parallel",),
                                             vmem_limit_bytes=VMEM_LIMIT),
        name="post",
    )(x2d, ya, yb, g, wgate, bgate, wua, wub, wout)


def _ffn_weights(w_gate, w_up, w_down):
    nc = D_FF // TF
    wg = w_gate.astype(BF16).reshape(D_MODEL, nc, TF).transpose(1, 0, 2)
    wu = w_up.astype(BF16).reshape(D_MODEL, nc, TF).transpose(1, 0, 2)
    wd = w_down.astype(BF16).reshape(nc, TF, D_MODEL)
    return wg, wu, wd


def kernel(x, norm_ffn1, w_ffn1_gate, w_ffn1_up, w_ffn1_down, norm_mix, w_in, b_forget, w_gate, b_gate,
           w_up_a, w_up_b, w_out, norm_ffn2, w_ffn2_gate, w_ffn2_up, w_ffn2_down, norm_final):
    batch, seq, _ = x.shape
    depth = norm_ffn1.shape[0]
    x2d = x.reshape(batch * seq, D_MODEL)
    gf = norm_final.reshape(1, D_MODEL)

    row = lax.broadcasted_iota(jnp.int32, (TM_PROJ, TM_PROJ), 0)
    col = lax.broadcasted_iota(jnp.int32, (TM_PROJ, TM_PROJ), 1)
    tri_l = (col <= row).astype(BF16)
    rk = lax.broadcasted_iota(jnp.int32, (TK, TK), 0)
    ck = lax.broadcasted_iota(jnp.int32, (TK, TK), 1)
    tri_u = jnp.tile((rk >= ck).astype(BF16), (2, 1))

    for l in range(depth):
        last = l == depth - 1
        x2d = _ffn(x2d, norm_ffn1[l].reshape(1, D_MODEL),
                   *_ffn_weights(w_ffn1_gate[l], w_ffn1_up[l], w_ffn1_down[l]), gf, final_norm=False)

        n_f = w_in.shape[2] - 6 * WIDTH
        w_in_p = jnp.pad(w_in[l], ((0, 0), (0, LANES - n_f))).astype(BF16)
        bf_p = jnp.pad(b_forget[l], (0, LANES - n_f)).reshape(1, LANES)
        g_mix = norm_mix[l].reshape(1, D_MODEL)
        qkv, cf_col, cf_row = _proj(x2d, g_mix, w_in_p, bf_p, tri_l, batch, seq)
        y_a = _sb_attn(qkv, tri_u, batch, seq)
        y_b = _fox_attn(qkv, cf_col, cf_row, batch, seq)
        x2d = _post(x2d, y_a, y_b, g_mix, w_gate[l].astype(BF16), b_gate[l].reshape(1, 2 * D_MODEL),
                    w_up_a[l].astype(BF16), w_up_b[l].astype(BF16), w_out[l].astype(BF16))

        x2d = _ffn(x2d, norm_ffn2[l].reshape(1, D_MODEL),
                   *_ffn_weights(w_ffn2_gate[l], w_ffn2_up[l], w_ffn2_down[l]), gf, final_norm=last)
    return x2d.reshape(batch, seq, D_MODEL)
```

```python
import functools

import jax
import jax.numpy as jnp
from jax import lax
from jax.experimental import pallas as pl
from jax.experimental.pallas import tpu as pltpu

D_MODEL = 1024
D_FF = 2816
HEAD_DIM = 64
N_HEADS = 8
WIDTH = N_HEADS * HEAD_DIM
RMS_EPS = 1e-6
ATTN_SCALE = HEAD_DIM ** -0.5
LOG2E = 1.4426950408889634

LANES = 128
HEADS_PER_BLOCK = LANES // HEAD_DIM
N_HEAD_BLOCKS = WIDTH // LANES

TM_FFN = 512
TF = 256
TM_PROJ = 512
TK = 256
TQ = 2 * TK
VMEM_LIMIT = 56 * 1024 * 1024

F32 = jnp.float32
BF16 = jnp.bfloat16
NEG_BIG = -1e30


def _rms(x, g):
    ms = jnp.mean(x * x, axis=-1, keepdims=True)
    return x * lax.rsqrt(ms + RMS_EPS) * g


def _const_spec(shape):
    nd = len(shape)
    return pl.BlockSpec(shape, lambda *_: (0,) * nd, pipeline_mode=pl.Buffered(1))


def _ffn_body(x_ref, g_ref, wg_ref, wu_ref, wd_ref, gf_ref, o_ref, *, final_norm):
    x = x_ref[...]
    xn = _rms(x, g_ref[...]).astype(BF16)
    acc = jnp.zeros(x.shape, F32)
    for c in range(D_FF // TF):
        sl = slice(c * TF, (c + 1) * TF)
        g = jnp.dot(xn, wg_ref[:, sl], preferred_element_type=F32)
        u = jnp.dot(xn, wu_ref[:, sl], preferred_element_type=F32)
        h = (g * jax.nn.sigmoid(g) * u).astype(BF16)
        acc = acc + jnp.dot(h, wd_ref[sl, :], preferred_element_type=F32)
    y = x + 0.5 * acc
    if final_norm:
        y = _rms(y, gf_ref[...])
    o_ref[...] = y


def _ffn(x2d, g, wg, wu, wd, gf, *, final_norm):
    t = x2d.shape[0]
    tile = pl.BlockSpec((TM_FFN, D_MODEL), lambda i: (i, 0))
    return pl.pallas_call(
        functools.partial(_ffn_body, final_norm=final_norm),
        grid=(t // TM_FFN,),
        in_specs=[tile, _const_spec((1, D_MODEL)), _const_spec((D_MODEL, D_FF)),
                  _const_spec((D_MODEL, D_FF)), _const_spec((D_FF, D_MODEL)),
                  _const_spec((1, D_MODEL))],
        out_specs=tile,
        out_shape=jax.ShapeDtypeStruct((t, D_MODEL), F32),
        compiler_params=pltpu.CompilerParams(dimension_semantics=("parallel",),
                                             vmem_limit_bytes=VMEM_LIMIT),
        name="ffn_final" if final_norm else "ffn",
    )(x2d, g, wg, wu, wd, gf)


def _proj_body(x_ref, g_ref, w_ref, bf_ref, tri_ref, qkv_ref, cfc_ref, cfr_ref, carry_ref):
    @pl.when(pl.program_id(1) == 0)
    def _():
        carry_ref[...] = jnp.zeros_like(carry_ref)

    h = _rms(x_ref[...], g_ref[...]).astype(BF16)
    for c in range(6):
        sl = slice(c * WIDTH, (c + 1) * WIDTH)
        p = jnp.dot(h, w_ref[:, sl], preferred_element_type=F32)
        if c == 0:
            p = p * (ATTN_SCALE * LOG2E)
        elif c == 3:
            p = p * ATTN_SCALE
        qkv_ref[:, sl] = p.astype(BF16)

    fl = jnp.dot(h, w_ref[:, 6 * WIDTH:], preferred_element_type=F32) + bf_ref[...]
    lf = jnp.minimum(fl, 0.0) - jnp.log1p(jnp.exp(-jnp.abs(fl)))
    hi = lf.astype(BF16)
    r1 = lf - hi.astype(F32)
    mid = r1.astype(BF16)
    lo = (r1 - mid.astype(F32)).astype(BF16)
    cs = jnp.dot(tri_ref[...], jnp.concatenate([hi, mid, lo], axis=1), preferred_element_type=F32)
    cf = cs[:, :LANES] + cs[:, LANES:2 * LANES] + cs[:, 2 * LANES:] + carry_ref[...]
    carry_ref[...] = cf[TM_PROJ - 1:TM_PROJ, :]
    cfc_ref[...] = cf
    cfr_ref[0] = cf.T[:N_HEADS, :]


def _proj(x2d, g, w, bf, tri, batch, seq):
    t = x2d.shape[0]
    ns = seq // TM_PROJ
    ncol = w.shape[1]
    return pl.pallas_call(
        _proj_body,
        grid=(batch, ns),
        in_specs=[pl.BlockSpec((TM_PROJ, D_MODEL), lambda b, s: (b * ns + s, 0)),
                  _const_spec((1, D_MODEL)), _const_spec((D_MODEL, ncol)),
                  _const_spec((1, LANES)), _const_spec((TM_PROJ, TM_PROJ))],
        out_specs=[pl.BlockSpec((TM_PROJ, 6 * WIDTH), lambda b, s: (b * ns + s, 0)),
                   pl.BlockSpec((TM_PROJ, LANES), lambda b, s: (b * ns + s, 0)),
                   pl.BlockSpec((1, N_HEADS, TM_PROJ), lambda b, s: (b, 0, s))],
        out_shape=[jax.ShapeDtypeStruct((t, 6 * WIDTH), BF16),
                   jax.ShapeDtypeStruct((t, LANES), F32),
                   jax.ShapeDtypeStruct((batch, N_HEADS, seq), F32)],
        scratch_shapes=[pltpu.VMEM((1, LANES), F32)],
        compiler_params=pltpu.CompilerParams(dimension_semantics=("arbitrary", "arbitrary"),
                                             vmem_limit_bytes=VMEM_LIMIT),
        name="proj",
    )(x2d, g, w, bf, tri)


def _head_split(q):
    lane = lax.broadcasted_iota(jnp.int32, q.shape, 1)
    zero = jnp.zeros_like(q)
    return [jnp.where(lane < HEAD_DIM, q, zero), jnp.where(lane >= HEAD_DIM, q, zero)]


def _qk(qh, kj):
    return lax.dot_general(qh, kj, (((1,), (1,)), ((), ())), preferred_element_type=F32)


def _attn_specs(batch, seq, col_q, col_k, col_v):
    nq = seq // TQ
    q_spec = pl.BlockSpec((TQ, LANES), lambda b, hp, i: (b * nq + i, col_q + hp))
    k_spec = pl.BlockSpec((seq, LANES), lambda b, hp, i: (b, col_k + hp))
    v_spec = pl.BlockSpec((seq, LANES), lambda b, hp, i: (b, col_v + hp))
    o_spec = pl.BlockSpec((TQ, LANES), lambda b, hp, i: (b * nq + i, hp))
    return nq, q_spec, k_spec, v_spec, o_spec


def _sb_body(q_ref, k_ref, v_ref, u_ref, o_ref, acc_ref):
    i = pl.program_id(2)
    qh = _head_split(q_ref[...])
    t_idx = lax.broadcasted_iota(jnp.int32, (TK, TK), 0)
    s_idx = lax.broadcasted_iota(jnp.int32, (TK, TK), 1)
    strict = s_idx < t_idx

    def blocks(tasks):
        offs = [pl.multiple_of(j * TK, TK) for j, _, _, _ in tasks]
        zs = [_qk(qh[h][rows], k_ref[pl.ds(off, TK), :]) for (_, h, rows, _), off in zip(tasks, offs)]
        cs = []
        for z, (_, _, _, diag) in zip(zs, tasks):
            sp = jnp.maximum(z, 0.0) + jnp.log2(1.0 + jnp.exp2(jnp.minimum(z, -z)))
            if diag:
                sp = jnp.where(strict, sp, 0.0)
            hi = sp.astype(BF16)
            lo = (sp - hi.astype(F32)).astype(BF16)
            cs.append(jnp.dot(jnp.concatenate([hi, lo], axis=1), u_ref[...], preferred_element_type=F32))
        outs = []
        for z, c, (_, _, _, diag), off in zip(zs, cs, tasks, offs):
            w = jnp.exp2(z - c)
            if diag:
                w = jnp.where(strict, w, 0.0)
            o = jnp.dot(w.astype(BF16), v_ref[pl.ds(off, TK), :], preferred_element_type=F32)
            outs.append((o, c[:, 0:1]))
        return outs

    lo_rows, hi_rows, all_rows = slice(0, TK), slice(TK, TQ), slice(0, TQ)
    heads = range(HEADS_PER_BLOCK)
    res = blocks([t for h in heads for t in ((2 * i + 1, h, hi_rows, True), (2 * i, h, hi_rows, False),
                                             (2 * i, h, lo_rows, True))])
    carry = []
    for h in heads:
        (o_b, r_b), (o_a, r_a), (o_l, r_l) = res[3 * h:3 * h + 3]
        acc_ref[h, hi_rows, :] = o_b + o_a * jnp.exp2(-r_b)
        acc_ref[h, lo_rows, :] = o_l
        carry.append(jnp.concatenate([r_l, r_a + r_b], axis=0))

    def step(jj, carry):
        j = 2 * (i - jj) - 1
        res = blocks([t for h in heads for t in ((j, h, all_rows, False), (j - 1, h, all_rows, False))])
        out = []
        for h in heads:
            (o1, t1), (o2, t2) = res[2 * h:2 * h + 2]
            r1 = carry[h]
            r2 = r1 + t1
            acc_ref[h] += o1 * jnp.exp2(-r1) + o2 * jnp.exp2(-r2)
            out.append(r2 + t2)
        return tuple(out)

    lax.fori_loop(0, i, step, tuple(carry))
    lane = lax.broadcasted_iota(jnp.int32, (TQ, LANES), 1)
    o_ref[...] = jnp.where(lane < HEAD_DIM, acc_ref[0], acc_ref[1]).astype(BF16)


def _sb_attn(qkv, tri_u, batch, seq):
    nq, q_spec, k_spec, v_spec, o_spec = _attn_specs(batch, seq, 0, N_HEAD_BLOCKS, 2 * N_HEAD_BLOCKS)
    return pl.pallas_call(
        _sb_body,
        grid=(batch, N_HEAD_BLOCKS, nq),
        in_specs=[q_spec, k_spec, v_spec, _const_spec((2 * TK, TK))],
        out_specs=o_spec,
        out_shape=jax.ShapeDtypeStruct((batch * seq, WIDTH), BF16),
        scratch_shapes=[pltpu.VMEM((HEADS_PER_BLOCK, TQ, LANES), F32)],
        compiler_params=pltpu.CompilerParams(dimension_semantics=("parallel", "parallel", "arbitrary"),
                                             vmem_limit_bytes=VMEM_LIMIT),
        name="sb_attn",
    )(qkv, qkv, qkv, tri_u)


def _lane_chunks(x):
    return [x[:, c * LANES:(c + 1) * LANES] for c in range(x.shape[1] // LANES)]


def _fox_body(q_ref, k_ref, v_ref, cfc_ref, cfr_ref, o_ref, s_ref, m_ref, l_ref, acc_ref):
    hp = pl.program_id(1)
    i = pl.program_id(2)
    qh = _head_split(q_ref[...])
    t_idx = lax.broadcasted_iota(jnp.int32, (TK, TK), 0)
    s_idx = lax.broadcasted_iota(jnp.int32, (TK, TK), 1)
    causal = s_idx <= t_idx
    lane = lax.broadcasted_iota(jnp.int32, (TQ, LANES), 1)
    cfc = cfc_ref[...]
    heads = [hp * HEADS_PER_BLOCK + h for h in range(HEADS_PER_BLOCK)]
    cf_t = [jnp.broadcast_to(jnp.sum(jnp.where(lane == hd, cfc, 0.0), axis=1, keepdims=True), (TQ, LANES))
            for hd in heads]
    lo_rows, hi_rows, all_rows = slice(0, TK), slice(TK, TQ), slice(0, TQ)
    diag_off = pl.multiple_of(i * TQ, TQ)

    def scores(off, width, h, rows):
        z = _qk(qh[h][rows], k_ref[pl.ds(off, width), :])
        cf_s = cfr_ref[0, pl.ds(heads[h], 1), pl.ds(off, width)]
        return jnp.concatenate([zc + cf_t[h][rows] for zc in _lane_chunks(z)], axis=1) - cf_s

    def chunk_max(s):
        return functools.reduce(jnp.maximum, _lane_chunks(s))

    for h in range(HEADS_PER_BLOCK):
        s_hi = scores(diag_off, TQ, h, hi_rows)
        s_hi = jnp.concatenate([s_hi[:, :TK], jnp.where(causal, s_hi[:, TK:], NEG_BIG)], axis=1)
        s_lo = jnp.where(causal, scores(diag_off, TK, h, lo_rows), NEG_BIG)
        s_ref[h, hi_rows, pl.ds(diag_off, TQ)] = s_hi
        s_ref[h, lo_rows, pl.ds(diag_off, TK)] = s_lo
        m_ref[h, hi_rows, :] = chunk_max(s_hi)
        m_ref[h, lo_rows, :] = chunk_max(s_lo)

    def pass1(jj, _):
        off = pl.multiple_of(jj * TQ, TQ)
        for h in range(HEADS_PER_BLOCK):
            s = scores(off, TQ, h, all_rows)
            s_ref[h, :, pl.ds(off, TQ)] = s
            m_ref[h] = jnp.maximum(m_ref[h], chunk_max(s))
        return 0

    lax.fori_loop(0, i, pass1, 0)
    for h in range(HEADS_PER_BLOCK):
        m_ref[h] = jnp.broadcast_to(jnp.max(m_ref[h], axis=1, keepdims=True), (TQ, LANES))

    def probs(off, width, h, rows):
        s = s_ref[h, rows, pl.ds(off, width)]
        m = m_ref[h, rows, :]
        pc = [jnp.exp(sc - m) for sc in _lane_chunks(s)]
        return jnp.concatenate(pc, axis=1).astype(BF16), functools.reduce(jnp.add, pc)

    def pv(p, off, width):
        return jnp.dot(p, v_ref[pl.ds(off, width), :], preferred_element_type=F32)

    for h in range(HEADS_PER_BLOCK):
        p_hi, l_hi = probs(diag_off, TQ, h, hi_rows)
        p_lo, l_lo = probs(diag_off, TK, h, lo_rows)
        acc_ref[h, hi_rows, :] = pv(p_hi, diag_off, TQ)
        acc_ref[h, lo_rows, :] = pv(p_lo, diag_off, TK)
        l_ref[h, hi_rows, :] = l_hi
        l_ref[h, lo_rows, :] = l_lo

    def pass2(jj, _):
        off = pl.multiple_of(jj * TQ, TQ)
        for h in range(HEADS_PER_BLOCK):
            p, l = probs(off, TQ, h, all_rows)
            acc_ref[h] += pv(p, off, TQ)
            l_ref[h] += l
        return 0

    lax.fori_loop(0, i, pass2, 0)
    y = [acc_ref[h] / jnp.sum(l_ref[h], axis=1, keepdims=True) for h in range(HEADS_PER_BLOCK)]
    o_ref[...] = jnp.where(lane < HEAD_DIM, y[0], y[1]).astype(BF16)


def _fox_attn(qkv, cf_col, cf_row, batch, seq):
    nq, q_spec, k_spec, v_spec, o_spec = _attn_specs(
        batch, seq, 3 * N_HEAD_BLOCKS, 4 * N_HEAD_BLOCKS, 5 * N_HEAD_BLOCKS)
    return pl.pallas_call(
        _fox_body,
        grid=(batch, N_HEAD_BLOCKS, nq),
        in_specs=[q_spec, k_spec, v_spec,
                  pl.BlockSpec((TQ, LANES), lambda b, hp, i: (b * nq + i, 0)),
                  pl.BlockSpec((1, N_HEADS, seq), lambda b, hp, i: (b, 0, 0))],
        out_specs=o_spec,
        out_shape=jax.ShapeDtypeStruct((batch * seq, WIDTH), BF16),
        scratch_shapes=[pltpu.VMEM((HEADS_PER_BLOCK, TQ, seq), F32)]
        + [pltpu.VMEM((HEADS_PER_BLOCK, TQ, LANES), F32)] * 3,
        compiler_params=pltpu.CompilerParams(dimension_semantics=("parallel", "parallel", "arbitrary"),
                                             vmem_limit_bytes=VMEM_LIMIT),
        name="fox_attn",
    )(qkv, qkv, qkv, cf_col, cf_row)


def _post_body(x_ref, ya_ref, yb_ref, g_ref, wgate_ref, bgate_ref, wua_ref, wub_ref, wout_ref, o_ref):
    x = x_ref[...]
    h = _rms(x, g_ref[...]).astype(BF16)
    ya = ya_ref[...]
    yb = yb_ref[...]
    acc = jnp.zeros(x.shape, F32)
    nchunk = D_MODEL // WIDTH
    for c in range(nchunk):
        sl_a = slice(c * WIDTH, (c + 1) * WIDTH)
        sl_b = slice(D_MODEL + c * WIDTH, D_MODEL + (c + 1) * WIDTH)
        ga = jax.nn.sigmoid(jnp.dot(h, wgate_ref[:, sl_a], preferred_element_type=F32) + bgate_ref[:, sl_a])
        gb = jax.nn.sigmoid(jnp.dot(h, wgate_ref[:, sl_b], preferred_element_type=F32) + bgate_ref[:, sl_b])
        ua = jnp.dot(ya, wua_ref[:, sl_a], preferred_element_type=F32)
        ub = jnp.dot(yb, wub_ref[:, sl_a], preferred_element_type=F32)
        mixed = (ga * ua + gb * ub).astype(BF16)
        acc = acc + jnp.dot(mixed, wout_ref[sl_a, :], preferred_element_type=F32)
    o_ref[...] = x + acc


def _post(x2d, ya, yb, g, wgate, bgate, wua, wub, wout):
    t = x2d.shape[0]
    tile = pl.BlockSpec((TM_FFN, D_MODEL), lambda i: (i, 0))
    ytile = pl.BlockSpec((TM_FFN, WIDTH), lambda i: (i, 0))
    return pl.pallas_call(
        _post_body,
        grid=(t // TM_FFN,),
        in_specs=[tile, ytile, ytile, _const_spec((1, D_MODEL)), _const_spec((D_MODEL, 2 * D_MODEL)),
                  _const_spec((1, 2 * D_MODEL)), _const_spec((WIDTH, D_MODEL)),
                  _const_spec((WIDTH, D_MODEL)), _const_spec((D_MODEL, D_MODEL))],
        out_specs=tile,
        out_shape=jax.ShapeDtypeStruct((t, D_MODEL), F32),
        compiler_params=pltpu.CompilerParams(dimension_semantics=("parallel",),
                                             vmem_limit_bytes=VMEM_LIMIT),
        name="post",
    )(x2d, ya, yb, g, wgate, bgate, wua, wub, wout)


def _ffn_weights(w_gate, w_up, w_down):
    return w_gate.astype(BF16), w_up.astype(BF16), w_down.astype(BF16)


def kernel(x, norm_ffn1, w_ffn1_gate, w_ffn1_up, w_ffn1_down, norm_mix, w_in, b_forget, w_gate, b_gate,
           w_up_a, w_up_b, w_out, norm_ffn2, w_ffn2_gate, w_ffn2_up, w_ffn2_down, norm_final):
    batch, seq, _ = x.shape
    depth = norm_ffn1.shape[0]
    x2d = x.reshape(batch * seq, D_MODEL)
    gf = norm_final.reshape(1, D_MODEL)

    row = lax.broadcasted_iota(jnp.int32, (TM_PROJ, TM_PROJ), 0)
    col = lax.broadcasted_iota(jnp.int32, (TM_PROJ, TM_PROJ), 1)
    tri_l = (col <= row).astype(BF16)
    rk = lax.broadcasted_iota(jnp.int32, (TK, TK), 0)
    ck = lax.broadcasted_iota(jnp.int32, (TK, TK), 1)
    tri_u = jnp.tile((rk >= ck).astype(BF16), (2, 1))

    for l in range(depth):
        last = l == depth - 1
        x2d = _ffn(x2d, norm_ffn1[l].reshape(1, D_MODEL),
                   *_ffn_weights(w_ffn1_gate[l], w_ffn1_up[l], w_ffn1_down[l]), gf, final_norm=False)

        n_f = w_in.shape[2] - 6 * WIDTH
        w_in_p = jnp.pad(w_in[l], ((0, 0), (0, LANES - n_f))).astype(BF16)
        bf_p = jnp.pad(b_forget[l], (0, LANES - n_f)).reshape(1, LANES)
        g_mix = norm_mix[l].reshape(1, D_MODEL)
        qkv, cf_col, cf_row = _proj(x2d, g_mix, w_in_p, bf_p, tri_l, batch, seq)
        y_a = _sb_attn(qkv, tri_u, batch, seq)
        y_b = _fox_attn(qkv, cf_col, cf_row, batch, seq)
        x2d = _post(x2d, y_a, y_b, g_mix, w_gate[l].astype(BF16), b_gate[l].reshape(1, 2 * D_MODEL),
                    w_up_a[l].astype(BF16), w_up_b[l].astype(BF16), w_out[l].astype(BF16))

        x2d = _ffn(x2d, norm_ffn2[l].reshape(1, D_MODEL),
                   *_ffn_weights(w_ffn2_gate[l], w_ffn2_up[l], w_ffn2_down[l]), gf, final_norm=last)
    return x2d.reshape(batch, seq, D_MODEL)
```

```python
import functools

import jax
import jax.numpy as jnp
from jax import lax
from jax.experimental import pallas as pl
from jax.experimental.pallas import tpu as pltpu

D_MODEL = 1024
D_FF = 2816
HEAD_DIM = 64
N_HEADS = 8
WIDTH = N_HEADS * HEAD_DIM
RMS_EPS = 1e-6
ATTN_SCALE = HEAD_DIM ** -0.5
LOG2E = 1.4426950408889634

LANES = 128
HEADS_PER_BLOCK = LANES // HEAD_DIM
N_HEAD_BLOCKS = WIDTH // LANES

TM_FFN = 512
TF = 256
TM_PROJ = 512
TK = 256
TQ = 2 * TK
VMEM_LIMIT = 56 * 1024 * 1024

F32 = jnp.float32
BF16 = jnp.bfloat16
NEG_BIG = -1e30
SB_MASS_CUTOFF = 160.0


def _rms(x, g):
    ms = jnp.mean(x * x, axis=-1, keepdims=True)
    return x * lax.rsqrt(ms + RMS_EPS) * g


def _const_spec(shape):
    nd = len(shape)
    return pl.BlockSpec(shape, lambda *_: (0,) * nd, pipeline_mode=pl.Buffered(1))


def _ffn_body(x_ref, g_ref, wg_ref, wu_ref, wd_ref, gf_ref, o_ref, *, final_norm):
    x = x_ref[...]
    xn = _rms(x, g_ref[...]).astype(BF16)
    acc = jnp.zeros(x.shape, F32)
    for c in range(D_FF // TF):
        sl = slice(c * TF, (c + 1) * TF)
        g = jnp.dot(xn, wg_ref[:, sl], preferred_element_type=F32)
        u = jnp.dot(xn, wu_ref[:, sl], preferred_element_type=F32)
        h = (g * jax.nn.sigmoid(g) * u).astype(BF16)
        acc = acc + jnp.dot(h, wd_ref[sl, :], preferred_element_type=F32)
    y = x + 0.5 * acc
    if final_norm:
        y = _rms(y, gf_ref[...])
    o_ref[...] = y


def _ffn(x2d, g, wg, wu, wd, gf, *, final_norm):
    t = x2d.shape[0]
    tile = pl.BlockSpec((TM_FFN, D_MODEL), lambda i: (i, 0))
    return pl.pallas_call(
        functools.partial(_ffn_body, final_norm=final_norm),
        grid=(t // TM_FFN,),
        in_specs=[tile, _const_spec((1, D_MODEL)), _const_spec((D_MODEL, D_FF)),
                  _const_spec((D_MODEL, D_FF)), _const_spec((D_FF, D_MODEL)),
                  _const_spec((1, D_MODEL))],
        out_specs=tile,
        out_shape=jax.ShapeDtypeStruct((t, D_MODEL), F32),
        compiler_params=pltpu.CompilerParams(dimension_semantics=("parallel",),
                                             vmem_limit_bytes=VMEM_LIMIT),
        name="ffn_final" if final_norm else "ffn",
    )(x2d, g, wg, wu, wd, gf)


def _proj_body(x_ref, g_ref, w_ref, bf_ref, tri_ref, qkv_ref, cfc_ref, cfr_ref, carry_ref):
    @pl.when(pl.program_id(1) == 0)
    def _():
        carry_ref[...] = jnp.zeros_like(carry_ref)

    h = _rms(x_ref[...], g_ref[...]).astype(BF16)

    def qkv_chunk(c):
        sl = slice(c * WIDTH, (c + 1) * WIDTH)
        p = jnp.dot(h, w_ref[:, sl], preferred_element_type=F32)
        if c in (0, 3):
            p = p * (ATTN_SCALE * LOG2E)
        qkv_ref[:, sl] = p.astype(BF16)

    fl = jnp.dot(h, w_ref[:, 6 * WIDTH:], preferred_element_type=F32) + bf_ref[...]
    lf = (jnp.minimum(fl, 0.0) - jnp.log1p(jnp.exp(-jnp.abs(fl)))) * LOG2E
    hi = lf.astype(BF16)
    r1 = lf - hi.astype(F32)
    mid = r1.astype(BF16)
    lo = (r1 - mid.astype(F32)).astype(BF16)
    for c in range(3):
        qkv_chunk(c)
    cs = jnp.dot(tri_ref[...], jnp.concatenate([hi, mid, lo], axis=1), preferred_element_type=F32)
    for c in range(3, 6):
        qkv_chunk(c)
    cf = cs[:, :LANES] + cs[:, LANES:2 * LANES] + cs[:, 2 * LANES:] + carry_ref[...]
    carry_ref[...] = cf[TM_PROJ - 1:TM_PROJ, :]
    cfc_ref[...] = cf
    cfr_ref[0] = cf.T[:N_HEADS, :]


def _proj(x2d, g, w, bf, tri, batch, seq):
    t = x2d.shape[0]
    ns = seq // TM_PROJ
    ncol = w.shape[1]
    return pl.pallas_call(
        _proj_body,
        grid=(batch, ns),
        in_specs=[pl.BlockSpec((TM_PROJ, D_MODEL), lambda b, s: (b * ns + s, 0)),
                  _const_spec((1, D_MODEL)), _const_spec((D_MODEL, ncol)),
                  _const_spec((1, LANES)), _const_spec((TM_PROJ, TM_PROJ))],
        out_specs=[pl.BlockSpec((TM_PROJ, 6 * WIDTH), lambda b, s: (b * ns + s, 0)),
                   pl.BlockSpec((TM_PROJ, LANES), lambda b, s: (b * ns + s, 0)),
                   pl.BlockSpec((1, N_HEADS, TM_PROJ), lambda b, s: (b, 0, s))],
        out_shape=[jax.ShapeDtypeStruct((t, 6 * WIDTH), BF16),
                   jax.ShapeDtypeStruct((t, LANES), F32),
                   jax.ShapeDtypeStruct((batch, N_HEADS, seq), F32)],
        scratch_shapes=[pltpu.VMEM((1, LANES), F32)],
        compiler_params=pltpu.CompilerParams(dimension_semantics=("arbitrary", "arbitrary"),
                                             vmem_limit_bytes=VMEM_LIMIT),
        name="proj",
    )(x2d, g, w, bf, tri)


def _head_split(q):
    lane = lax.broadcasted_iota(jnp.int32, q.shape, 1)
    zero = jnp.zeros_like(q)
    return [jnp.where(lane < HEAD_DIM, q, zero), jnp.where(lane >= HEAD_DIM, q, zero)]


def _qk(qh, kj):
    return lax.dot_general(qh, kj, (((1,), (1,)), ((), ())), preferred_element_type=F32)


def _attn_specs(batch, seq, col_q, col_k, col_v):
    nq = seq // TQ
    q_spec = pl.BlockSpec((TQ, LANES), lambda b, hp, i: (b * nq + i, col_q + hp))
    k_spec = pl.BlockSpec((seq, LANES), lambda b, hp, i: (b, col_k + hp))
    v_spec = pl.BlockSpec((seq, LANES), lambda b, hp, i: (b, col_v + hp))
    o_spec = pl.BlockSpec((TQ, LANES), lambda b, hp, i: (b * nq + i, hp))
    return nq, q_spec, k_spec, v_spec, o_spec


def _sb_body(q_ref, k_ref, v_ref, u_ref, o_ref, acc_ref):
    i = pl.program_id(2)
    qh = _head_split(q_ref[...])
    t_idx = lax.broadcasted_iota(jnp.int32, (TK, TK), 0)
    s_idx = lax.broadcasted_iota(jnp.int32, (TK, TK), 1)
    strict = s_idx < t_idx

    def blocks(tasks):
        offs = [pl.multiple_of(j * TK, TK) for j, _, _, _ in tasks]
        zs = [_qk(qh[h][rows], k_ref[pl.ds(off, TK), :]) for (_, h, rows, _), off in zip(tasks, offs)]
        cs = []
        for z, (_, _, _, diag) in zip(zs, tasks):
            sp = jnp.maximum(z, 0.0) + jnp.log2(1.0 + jnp.exp2(jnp.minimum(z, -z)))
            if diag:
                sp = jnp.where(strict, sp, 0.0)
            hi = sp.astype(BF16)
            lo = (sp - hi.astype(F32)).astype(BF16)
            cs.append(jnp.dot(jnp.concatenate([hi, lo], axis=1), u_ref[...], preferred_element_type=F32))
        outs = []
        for z, c, (_, _, _, diag), off in zip(zs, cs, tasks, offs):
            w = jnp.exp2(z - c)
            if diag:
                w = jnp.where(strict, w, 0.0)
            o = jnp.dot(w.astype(BF16), v_ref[pl.ds(off, TK), :], preferred_element_type=F32)
            outs.append((o, c[:, 0:1]))
        return outs

    lo_rows, hi_rows = slice(0, TK), slice(TK, TQ)
    heads = range(HEADS_PER_BLOCK)
    res = blocks([t for h in heads for t in ((2 * i + 1, h, hi_rows, True), (2 * i, h, hi_rows, False),
                                             (2 * i, h, lo_rows, True))])
    mass = []
    for h in heads:
        (o_b, r_b), (o_a, r_a), (o_l, r_l) = res[3 * h:3 * h + 3]
        acc_ref[h, hi_rows, :] = o_b + o_a * jnp.exp2(-r_b)
        acc_ref[h, lo_rows, :] = o_l
        mass.append(jnp.concatenate([r_l, r_a + r_b], axis=0))

    def min_mass(mass):
        return jnp.min(functools.reduce(jnp.minimum, mass))

    def live(carry):
        j, lowest = carry[0], carry[1]
        return jnp.logical_and(j >= 0, lowest < SB_MASS_CUTOFF)

    def step(carry):
        j, mass = carry[0], carry[2:]
        halves = [(j, h, rows, False) for h in heads for rows in (lo_rows, hi_rows)]
        res = blocks(halves)
        new_mass = []
        for h in heads:
            (o_l, t_l), (o_h, t_h) = res[2 * h:2 * h + 2]
            o, t = jnp.concatenate([o_l, o_h], axis=0), jnp.concatenate([t_l, t_h], axis=0)
            acc_ref[h] += o * jnp.exp2(-mass[h])
            new_mass.append(mass[h] + t)
        return (j - 1, min_mass(new_mass), *new_mass)

    lax.while_loop(live, step, (2 * i - 1, min_mass(mass), *mass))
    lane = lax.broadcasted_iota(jnp.int32, (TQ, LANES), 1)
    o_ref[...] = jnp.where(lane < HEAD_DIM, acc_ref[0], acc_ref[1]).astype(BF16)


def _sb_attn(qkv, tri_u, batch, seq):
    nq, q_spec, k_spec, v_spec, o_spec = _attn_specs(batch, seq, 0, N_HEAD_BLOCKS, 2 * N_HEAD_BLOCKS)
    return pl.pallas_call(
        _sb_body,
        grid=(batch, N_HEAD_BLOCKS, nq),
        in_specs=[q_spec, k_spec, v_spec, _const_spec((2 * TK, TK))],
        out_specs=o_spec,
        out_shape=jax.ShapeDtypeStruct((batch * seq, WIDTH), BF16),
        scratch_shapes=[pltpu.VMEM((HEADS_PER_BLOCK, TQ, LANES), F32)],
        compiler_params=pltpu.CompilerParams(dimension_semantics=("parallel", "parallel", "arbitrary"),
                                             vmem_limit_bytes=VMEM_LIMIT),
        name="sb_attn",
    )(qkv, qkv, qkv, tri_u)


def _lane_chunks(x):
    return [x[:, c * LANES:(c + 1) * LANES] for c in range(x.shape[1] // LANES)]


def _fox_body(q_ref, k_ref, v_ref, cfc_ref, cfr_ref, o_ref, s_ref, m_ref, l_ref, acc_ref):
    hp = pl.program_id(1)
    i = pl.program_id(2)
    qh = _head_split(q_ref[...])
    t_idx = lax.broadcasted_iota(jnp.int32, (TK, TK), 0)
    s_idx = lax.broadcasted_iota(jnp.int32, (TK, TK), 1)
    causal = s_idx <= t_idx
    lane = lax.broadcasted_iota(jnp.int32, (TQ, LANES), 1)
    cfc = cfc_ref[...]
    heads = [hp * HEADS_PER_BLOCK + h for h in range(HEADS_PER_BLOCK)]
    cf_t = [jnp.broadcast_to(jnp.sum(jnp.where(lane == hd, cfc, 0.0), axis=1, keepdims=True), (TQ, LANES))
            for hd in heads]
    lo_rows, hi_rows, all_rows = slice(0, TK), slice(TK, TQ), slice(0, TQ)
    diag_off = pl.multiple_of(i * TQ, TQ)

    def scores(off, width, h, rows):
        z = _qk(qh[h][rows], k_ref[pl.ds(off, width), :])
        cf_s = cfr_ref[0, pl.ds(heads[h], 1), pl.ds(off, width)]
        return jnp.concatenate([zc + cf_t[h][rows] for zc in _lane_chunks(z)], axis=1) - cf_s

    def chunk_max(s):
        return functools.reduce(jnp.maximum, _lane_chunks(s))

    for h in range(HEADS_PER_BLOCK):
        s_hi = scores(diag_off, TQ, h, hi_rows)
        s_hi = jnp.concatenate([s_hi[:, :TK], jnp.where(causal, s_hi[:, TK:], NEG_BIG)], axis=1)
        s_lo = jnp.where(causal, scores(diag_off, TK, h, lo_rows), NEG_BIG)
        s_ref[h, hi_rows, pl.ds(diag_off, TQ)] = s_hi
        s_ref[h, lo_rows, pl.ds(diag_off, TK)] = s_lo
        m_ref[h, hi_rows, :] = chunk_max(s_hi)
        m_ref[h, lo_rows, :] = chunk_max(s_lo)

    def pass1(jj, _):
        off = pl.multiple_of(jj * TQ, TQ)
        for h in range(HEADS_PER_BLOCK):
            s = scores(off, TQ, h, all_rows)
            s_ref[h, :, pl.ds(off, TQ)] = s
            m_ref[h] = jnp.maximum(m_ref[h], chunk_max(s))
        return 0

    lax.fori_loop(0, i, pass1, 0)
    for h in range(HEADS_PER_BLOCK):
        m_ref[h] = jnp.broadcast_to(jnp.max(m_ref[h], axis=1, keepdims=True), (TQ, LANES))

    def probs(off, width, h, rows):
        s = s_ref[h, rows, pl.ds(off, width)]
        m = m_ref[h, rows, :]
        pc = [jnp.exp2(sc - m) for sc in _lane_chunks(s)]
        return jnp.concatenate(pc, axis=1).astype(BF16), functools.reduce(jnp.add, pc)

    def pv(p, off, width):
        return jnp.dot(p, v_ref[pl.ds(off, width), :], preferred_element_type=F32)

    for h in range(HEADS_PER_BLOCK):
        p_hi, l_hi = probs(diag_off, TQ, h, hi_rows)
        p_lo, l_lo = probs(diag_off, TK, h, lo_rows)
        acc_ref[h, hi_rows, :] = pv(p_hi, diag_off, TQ)
        acc_ref[h, lo_rows, :] = pv(p_lo, diag_off, TK)
        l_ref[h, hi_rows, :] = l_hi
        l_ref[h, lo_rows, :] = l_lo

    def pass2(jj, _):
        off = pl.multiple_of(jj * TQ, TQ)
        for h in range(HEADS_PER_BLOCK):
            p, l = probs(off, TQ, h, all_rows)
            acc_ref[h] += pv(p, off, TQ)
            l_ref[h] += l
        return 0

    lax.fori_loop(0, i, pass2, 0)
    y = [acc_ref[h] / jnp.sum(l_ref[h], axis=1, keepdims=True) for h in range(HEADS_PER_BLOCK)]
    o_ref[...] = jnp.where(lane < HEAD_DIM, y[0], y[1]).astype(BF16)


def _fox_attn(qkv, cf_col, cf_row, batch, seq):
    nq, q_spec, k_spec, v_spec, o_spec = _attn_specs(
        batch, seq, 3 * N_HEAD_BLOCKS, 4 * N_HEAD_BLOCKS, 5 * N_HEAD_BLOCKS)
    return pl.pallas_call(
        _fox_body,
        grid=(batch, N_HEAD_BLOCKS, nq),
        in_specs=[q_spec, k_spec, v_spec,
                  pl.BlockSpec((TQ, LANES), lambda b, hp, i: (b * nq + i, 0)),
                  pl.BlockSpec((1, N_HEADS, seq), lambda b, hp, i: (b, 0, 0))],
        out_specs=o_spec,
        out_shape=jax.ShapeDtypeStruct((batch * seq, WIDTH), BF16),
        scratch_shapes=[pltpu.VMEM((HEADS_PER_BLOCK, TQ, seq), F32)]
        + [pltpu.VMEM((HEADS_PER_BLOCK, TQ, LANES), F32)] * 3,
        compiler_params=pltpu.CompilerParams(dimension_semantics=("parallel", "parallel", "arbitrary"),
                                             vmem_limit_bytes=VMEM_LIMIT),
        name="fox_attn",
    )(qkv, qkv, qkv, cf_col, cf_row)


def _post_body(x_ref, ya_ref, yb_ref, g_ref, wgate_ref, bgate_ref, wua_ref, wub_ref, wout_ref, o_ref):
    x = x_ref[...]
    h = _rms(x, g_ref[...]).astype(BF16)
    ya = ya_ref[...]
    yb = yb_ref[...]
    acc = jnp.zeros(x.shape, F32)
    nchunk = D_MODEL // WIDTH
    for c in range(nchunk):
        sl_a = slice(c * WIDTH, (c + 1) * WIDTH)
        sl_b = slice(D_MODEL + c * WIDTH, D_MODEL + (c + 1) * WIDTH)
        ga = jax.nn.sigmoid(jnp.dot(h, wgate_ref[:, sl_a], preferred_element_type=F32) + bgate_ref[:, sl_a])
        gb = jax.nn.sigmoid(jnp.dot(h, wgate_ref[:, sl_b], preferred_element_type=F32) + bgate_ref[:, sl_b])
        ua = jnp.dot(ya, wua_ref[:, sl_a], preferred_element_type=F32)
        ub = jnp.dot(yb, wub_ref[:, sl_a], preferred_element_type=F32)
        mixed = (ga * ua + gb * ub).astype(BF16)
        acc = acc + jnp.dot(mixed, wout_ref[sl_a, :], preferred_element_type=F32)
    o_ref[...] = x + acc


def _post(x2d, ya, yb, g, wgate, bgate, wua, wub, wout):
    t = x2d.shape[0]
    tile = pl.BlockSpec((TM_FFN, D_MODEL), lambda i: (i, 0))
    ytile = pl.BlockSpec((TM_FFN, WIDTH), lambda i: (i, 0))
    return pl.pallas_call(
        _post_body,
        grid=(t // TM_FFN,),
        in_specs=[tile, ytile, ytile, _const_spec((1, D_MODEL)), _const_spec((D_MODEL, 2 * D_MODEL)),
                  _const_spec((1, 2 * D_MODEL)), _const_spec((WIDTH, D_MODEL)),
                  _const_spec((WIDTH, D_MODEL)), _const_spec((D_MODEL, D_MODEL))],
        out_specs=tile,
        out_shape=jax.ShapeDtypeStruct((t, D_MODEL), F32),
        compiler_params=pltpu.CompilerParams(dimension_semantics=("parallel",),
                                             vmem_limit_bytes=VMEM_LIMIT),
        name="post",
    )(x2d, ya, yb, g, wgate, bgate, wua, wub, wout)


def _ffn_weights(w_gate, w_up, w_down):
    return w_gate.astype(BF16), w_up.astype(BF16), w_down.astype(BF16)


def kernel(x, norm_ffn1, w_ffn1_gate, w_ffn1_up, w_ffn1_down, norm_mix, w_in, b_forget, w_gate, b_gate,
           w_up_a, w_up_b, w_out, norm_ffn2, w_ffn2_gate, w_ffn2_up, w_ffn2_down, norm_final):
    batch, seq, _ = x.shape
    depth = norm_ffn1.shape[0]
    x2d = x.reshape(batch * seq, D_MODEL)
    gf = norm_final.reshape(1, D_MODEL)

    row = lax.broadcasted_iota(jnp.int32, (TM_PROJ, TM_PROJ), 0)
    col = lax.broadcasted_iota(jnp.int32, (TM_PROJ, TM_PROJ), 1)
    tri_l = (col <= row).astype(BF16)
    rk = lax.broadcasted_iota(jnp.int32, (TK, TK), 0)
    ck = lax.broadcasted_iota(jnp.int32, (TK, TK), 1)
    tri_u = jnp.tile((rk >= ck).astype(BF16), (2, 1))

    for l in range(depth):
        last = l == depth - 1
        x2d = _ffn(x2d, norm_ffn1[l].reshape(1, D_MODEL),
                   *_ffn_weights(w_ffn1_gate[l], w_ffn1_up[l], w_ffn1_down[l]), gf, final_norm=False)

        n_f = w_in.shape[2] - 6 * WIDTH
        w_in_p = jnp.pad(w_in[l], ((0, 0), (0, LANES - n_f))).astype(BF16)
        bf_p = jnp.pad(b_forget[l], (0, LANES - n_f)).reshape(1, LANES)
        g_mix = norm_mix[l].reshape(1, D_MODEL)
        qkv, cf_col, cf_row = _proj(x2d, g_mix, w_in_p, bf_p, tri_l, batch, seq)
        y_a = _sb_attn(qkv, tri_u, batch, seq)
        y_b = _fox_attn(qkv, cf_col, cf_row, batch, seq)
        x2d = _post(x2d, y_a, y_b, g_mix, w_gate[l].astype(BF16), b_gate[l].reshape(1, 2 * D_MODEL),
                    w_up_a[l].astype(BF16), w_up_b[l].astype(BF16), w_out[l].astype(BF16))

        x2d = _ffn(x2d, norm_ffn2[l].reshape(1, D_MODEL),
                   *_ffn_weights(w_ffn2_gate[l], w_ffn2_up[l], w_ffn2_down[l]), gf, final_norm=last)
    return x2d.reshape(batch, seq, D_MODEL)
```

```python
import functools

import jax
import jax.numpy as jnp
from jax import lax
from jax.experimental import pallas as pl
from jax.experimental.pallas import tpu as pltpu

D_MODEL = 1024
D_FF = 2816
HEAD_DIM = 64
N_HEADS = 8
WIDTH = N_HEADS * HEAD_DIM
RMS_EPS = 1e-6
ATTN_SCALE = HEAD_DIM ** -0.5
LOG2E = 1.4426950408889634

LANES = 128
HEADS_PER_BLOCK = LANES // HEAD_DIM
N_HEAD_BLOCKS = WIDTH // LANES

TM_FFN = 512
TF = 256
TM_PROJ = 512
TK = 256
TQ = 2 * TK
VMEM_LIMIT = 56 * 1024 * 1024

F32 = jnp.float32
BF16 = jnp.bfloat16
NEG_BIG = -1e30
SB_MASS_CUTOFF = 160.0


def _rms(x, g):
    ms = jnp.mean(x * x, axis=-1, keepdims=True)
    return x * lax.rsqrt(ms + RMS_EPS) * g


def _const_spec(shape):
    nd = len(shape)
    return pl.BlockSpec(shape, lambda *_: (0,) * nd, pipeline_mode=pl.Buffered(1))


def _ffn_body(x_ref, g_ref, wg_ref, wu_ref, wd_ref, gf_ref, o_ref, *, final_norm):
    x = x_ref[...]
    xn = _rms(x, g_ref[...]).astype(BF16)
    acc = jnp.zeros(x.shape, F32)
    for c in range(D_FF // TF):
        sl = slice(c * TF, (c + 1) * TF)
        g = jnp.dot(xn, wg_ref[:, sl].astype(BF16), preferred_element_type=F32)
        u = jnp.dot(xn, wu_ref[:, sl].astype(BF16), preferred_element_type=F32)
        h = (g * jax.nn.sigmoid(g) * u).astype(BF16)
        acc = acc + jnp.dot(h, wd_ref[sl, :].astype(BF16), preferred_element_type=F32)
    y = x + 0.5 * acc
    if final_norm:
        y = _rms(y, gf_ref[...])
    o_ref[...] = y


def _ffn(x2d, g, wg, wu, wd, gf, *, final_norm):
    t = x2d.shape[0]
    tile = pl.BlockSpec((TM_FFN, D_MODEL), lambda i: (i, 0))
    return pl.pallas_call(
        functools.partial(_ffn_body, final_norm=final_norm),
        grid=(t // TM_FFN,),
        in_specs=[tile, _const_spec((1, D_MODEL)), _const_spec((D_MODEL, D_FF)),
                  _const_spec((D_MODEL, D_FF)), _const_spec((D_FF, D_MODEL)),
                  _const_spec((1, D_MODEL))],
        out_specs=tile,
        out_shape=jax.ShapeDtypeStruct((t, D_MODEL), F32),
        compiler_params=pltpu.CompilerParams(dimension_semantics=("parallel",),
                                             vmem_limit_bytes=VMEM_LIMIT),
        name="ffn_final" if final_norm else "ffn",
    )(x2d, g, wg, wu, wd, gf)


def _proj_body(x_ref, g_ref, w_ref, wf_ref, bf_ref, tri_ref, qkv_ref, cfc_ref, cfr_ref, carry_ref):
    @pl.when(pl.program_id(1) == 0)
    def _():
        carry_ref[...] = jnp.zeros_like(carry_ref)

    h = _rms(x_ref[...], g_ref[...]).astype(BF16)

    def qkv_chunk(c):
        sl = slice(c * WIDTH, (c + 1) * WIDTH)
        p = jnp.dot(h, w_ref[:, sl].astype(BF16), preferred_element_type=F32)
        if c in (0, 3):
            p = p * (ATTN_SCALE * LOG2E)
        qkv_ref[:, sl] = p.astype(BF16)

    fl = jnp.dot(h, wf_ref[...], preferred_element_type=F32) + bf_ref[...]
    lf = (jnp.minimum(fl, 0.0) - jnp.log1p(jnp.exp(-jnp.abs(fl)))) * LOG2E
    hi = lf.astype(BF16)
    r1 = lf - hi.astype(F32)
    mid = r1.astype(BF16)
    lo = (r1 - mid.astype(F32)).astype(BF16)
    for c in range(3):
        qkv_chunk(c)
    cs = jnp.dot(tri_ref[...], jnp.concatenate([hi, mid, lo], axis=1), preferred_element_type=F32)
    for c in range(3, 6):
        qkv_chunk(c)
    cf = cs[:, :LANES] + cs[:, LANES:2 * LANES] + cs[:, 2 * LANES:] + carry_ref[...]
    carry_ref[...] = cf[TM_PROJ - 1:TM_PROJ, :]
    cfc_ref[...] = cf
    cfr_ref[0] = cf.T[:N_HEADS, :]


def _proj(x2d, g, w, wf, bf, tri, batch, seq):
    t = x2d.shape[0]
    ns = seq // TM_PROJ
    return pl.pallas_call(
        _proj_body,
        grid=(batch, ns),
        in_specs=[pl.BlockSpec((TM_PROJ, D_MODEL), lambda b, s: (b * ns + s, 0)),
                  _const_spec((1, D_MODEL)), _const_spec(w.shape), _const_spec((D_MODEL, LANES)),
                  _const_spec((1, LANES)), _const_spec((TM_PROJ, TM_PROJ))],
        out_specs=[pl.BlockSpec((TM_PROJ, 6 * WIDTH), lambda b, s: (b * ns + s, 0)),
                   pl.BlockSpec((TM_PROJ, LANES), lambda b, s: (b * ns + s, 0)),
                   pl.BlockSpec((1, N_HEADS, TM_PROJ), lambda b, s: (b, 0, s))],
        out_shape=[jax.ShapeDtypeStruct((t, 6 * WIDTH), BF16),
                   jax.ShapeDtypeStruct((t, LANES), F32),
                   jax.ShapeDtypeStruct((batch, N_HEADS, seq), F32)],
        scratch_shapes=[pltpu.VMEM((1, LANES), F32)],
        compiler_params=pltpu.CompilerParams(dimension_semantics=("arbitrary", "arbitrary"),
                                             vmem_limit_bytes=VMEM_LIMIT),
        name="proj",
    )(x2d, g, w, wf, bf, tri)


def _head_split(q):
    lane = lax.broadcasted_iota(jnp.int32, q.shape, 1)
    zero = jnp.zeros_like(q)
    return [jnp.where(lane < HEAD_DIM, q, zero), jnp.where(lane >= HEAD_DIM, q, zero)]


def _qk(qh, kj):
    return lax.dot_general(qh, kj, (((1,), (1,)), ((), ())), preferred_element_type=F32)


def _lane_chunks(x):
    return [x[:, c * LANES:(c + 1) * LANES] for c in range(x.shape[1] // LANES)]


def _attn_body(qa_ref, ka_ref, va_ref, qb_ref, kb_ref, vb_ref, u_ref, cfc_ref, cfr_ref, ya_ref, yb_ref,
               acc_a, s_ref, m_ref, l_ref, acc_b):
    hp = pl.program_id(1)
    i = pl.program_id(2)
    heads = range(HEADS_PER_BLOCK)
    lo_rows, hi_rows, all_rows = slice(0, TK), slice(TK, TQ), slice(0, TQ)
    t_idx = lax.broadcasted_iota(jnp.int32, (TK, TK), 0)
    s_idx = lax.broadcasted_iota(jnp.int32, (TK, TK), 1)
    strict = s_idx < t_idx
    causal = s_idx <= t_idx
    lane = lax.broadcasted_iota(jnp.int32, (TQ, LANES), 1)
    diag_off = pl.multiple_of(i * TQ, TQ)

    qa = _head_split(qa_ref[...])

    def key_off(j):
        return pl.multiple_of(j * TK, TK)

    def sb_qk(tasks):
        return [_qk(qa[h][rows], ka_ref[pl.ds(key_off(j), TK), :]) for j, h, rows, _ in tasks]

    def sb_cumsum(tasks, zs):
        cs = []
        for z, (_, _, _, diag) in zip(zs, tasks):
            sp = jnp.maximum(z, 0.0) + jnp.log2(1.0 + jnp.exp2(jnp.minimum(z, -z)))
            if diag:
                sp = jnp.where(strict, sp, 0.0)
            hi = sp.astype(BF16)
            lo = (sp - hi.astype(F32)).astype(BF16)
            cs.append(jnp.dot(jnp.concatenate([hi, lo], axis=1), u_ref[...], preferred_element_type=F32))
        return cs

    def sb_pv(tasks, zs, cs):
        outs = []
        for z, c, (j, _, _, diag) in zip(zs, cs, tasks):
            w = jnp.exp2(z - c)
            if diag:
                w = jnp.where(strict, w, 0.0)
            o = jnp.dot(w.astype(BF16), va_ref[pl.ds(key_off(j), TK), :], preferred_element_type=F32)
            outs.append((o, c[:, 0:1]))
        return outs

    def sb_block_tasks(j):
        return [(j, h, rows, False) for h in heads for rows in (lo_rows, hi_rows)]

    def sb_add_block(res, mass, gate=None):
        new_mass = []
        for h in heads:
            (o_l, t_l), (o_h, t_h) = res[2 * h:2 * h + 2]
            o, t = jnp.concatenate([o_l, o_h], axis=0), jnp.concatenate([t_l, t_h], axis=0)
            scale = jnp.exp2(-mass[h])
            if gate is not None:
                scale, t = scale * gate, t * gate
            acc_a[h] += o * scale
            new_mass.append(mass[h] + t)
        return new_mass

    qb = _head_split(qb_ref[...])
    cfc = cfc_ref[...]
    fox_heads = [hp * HEADS_PER_BLOCK + h for h in heads]
    cf_t = [jnp.broadcast_to(jnp.sum(jnp.where(lane == hd, cfc, 0.0), axis=1, keepdims=True), (TQ, LANES))
            for hd in fox_heads]

    def fox_qk(off, width, h, rows):
        return _qk(qb[h][rows], kb_ref[pl.ds(off, width), :])

    def fox_logits(z, off, width, h, rows):
        cf_s = cfr_ref[0, pl.ds(fox_heads[h], 1), pl.ds(off, width)]
        return jnp.concatenate([zc + cf_t[h][rows] for zc in _lane_chunks(z)], axis=1) - cf_s

    def chunk_max(s):
        return functools.reduce(jnp.maximum, _lane_chunks(s))

    def fox_probs(off, width, h, rows):
        s = s_ref[h, rows, pl.ds(off, width)]
        m = m_ref[h, rows, :]
        pc = [jnp.exp2(sc - m) for sc in _lane_chunks(s)]
        return jnp.concatenate(pc, axis=1).astype(BF16), functools.reduce(jnp.add, pc)

    def fox_pv(p, off, width):
        return jnp.dot(p, vb_ref[pl.ds(off, width), :], preferred_element_type=F32)

    diag_tasks = [t for h in heads for t in ((2 * i + 1, h, hi_rows, True), (2 * i, h, hi_rows, False),
                                             (2 * i, h, lo_rows, True))]
    zs = sb_qk(diag_tasks)
    fz = [(fox_qk(diag_off, TQ, h, hi_rows), fox_qk(diag_off, TK, h, lo_rows)) for h in heads]
    cs = sb_cumsum(diag_tasks, zs)
    for h in heads:
        s_hi = fox_logits(fz[h][0], diag_off, TQ, h, hi_rows)
        s_hi = jnp.concatenate([s_hi[:, :TK], jnp.where(causal, s_hi[:, TK:], NEG_BIG)], axis=1)
        s_lo = jnp.where(causal, fox_logits(fz[h][1], diag_off, TK, h, lo_rows), NEG_BIG)
        s_ref[h, hi_rows, pl.ds(diag_off, TQ)] = s_hi
        s_ref[h, lo_rows, pl.ds(diag_off, TK)] = s_lo
        m_ref[h, hi_rows, :] = chunk_max(s_hi)
        m_ref[h, lo_rows, :] = chunk_max(s_lo)
    res = sb_pv(diag_tasks, zs, cs)
    mass = []
    for h in heads:
        (o_b, r_b), (o_a, r_a), (o_l, r_l) = res[3 * h:3 * h + 3]
        acc_a[h, hi_rows, :] = o_b + o_a * jnp.exp2(-r_b)
        acc_a[h, lo_rows, :] = o_l
        mass.append(jnp.concatenate([r_l, r_a + r_b], axis=0))

    def pass1(jj, _):
        off = pl.multiple_of(jj * TQ, TQ)
        for h in heads:
            s = fox_logits(fox_qk(off, TQ, h, all_rows), off, TQ, h, all_rows)
            s_ref[h, :, pl.ds(off, TQ)] = s
            m_ref[h] = jnp.maximum(m_ref[h], chunk_max(s))
        return 0

    lax.fori_loop(0, i, pass1, 0)

    for h in heads:
        m_ref[h] = jnp.broadcast_to(jnp.max(m_ref[h], axis=1, keepdims=True), (TQ, LANES))
    first_tasks = sb_block_tasks(jnp.maximum(2 * i - 1, 0))
    zs = sb_qk(first_tasks)
    probs = [(fox_probs(diag_off, TQ, h, hi_rows), fox_probs(diag_off, TK, h, lo_rows)) for h in heads]
    for h in heads:
        (p_hi, l_hi), (p_lo, l_lo) = probs[h]
        acc_b[h, hi_rows, :] = fox_pv(p_hi, diag_off, TQ)
        acc_b[h, lo_rows, :] = fox_pv(p_lo, diag_off, TK)
        l_ref[h, hi_rows, :] = l_hi
        l_ref[h, lo_rows, :] = l_lo
    cs = sb_cumsum(first_tasks, zs)
    res = sb_pv(first_tasks, zs, cs)
    mass = sb_add_block(res, mass, gate=(i > 0).astype(F32))

    def pass2(jj, _):
        off = pl.multiple_of(jj * TQ, TQ)
        for h in heads:
            p, l = fox_probs(off, TQ, h, all_rows)
            acc_b[h] += fox_pv(p, off, TQ)
            l_ref[h] += l
        return 0

    lax.fori_loop(0, i, pass2, 0)

    def min_mass(mass):
        return jnp.min(functools.reduce(jnp.minimum, mass))

    def live(carry):
        return jnp.logical_and(carry[0] >= 0, carry[1] < SB_MASS_CUTOFF)

    def step(carry):
        j, mass = carry[0], carry[2:]
        tasks = sb_block_tasks(j)
        zs = sb_qk(tasks)
        new_mass = sb_add_block(sb_pv(tasks, zs, sb_cumsum(tasks, zs)), mass)
        return (j - 1, min_mass(new_mass), *new_mass)

    lax.while_loop(live, step, (2 * i - 2, min_mass(mass), *mass))

    ya_ref[...] = jnp.where(lane < HEAD_DIM, acc_a[0], acc_a[1]).astype(BF16)
    yb = [acc_b[h] / jnp.sum(l_ref[h], axis=1, keepdims=True) for h in heads]
    yb_ref[...] = jnp.where(lane < HEAD_DIM, yb[0], yb[1]).astype(BF16)


def _attention(qkv, tri_u, cf_col, cf_row, batch, seq):
    nq = seq // TQ
    nb = N_HEAD_BLOCKS

    def q_spec(col):
        return pl.BlockSpec((TQ, LANES), lambda b, hp, i: (b * nq + i, col + hp))

    def kv_spec(col):
        return pl.BlockSpec((seq, LANES), lambda b, hp, i: (b, col + hp))

    y_spec = pl.BlockSpec((TQ, LANES), lambda b, hp, i: (b * nq + i, hp))
    y_shape = jax.ShapeDtypeStruct((batch * seq, WIDTH), BF16)
    pair_tile = pltpu.VMEM((HEADS_PER_BLOCK, TQ, LANES), F32)
    return pl.pallas_call(
        _attn_body,
        grid=(batch, nb, nq),
        in_specs=[q_spec(0), kv_spec(nb), kv_spec(2 * nb), q_spec(3 * nb), kv_spec(4 * nb), kv_spec(5 * nb),
                  _const_spec((2 * TK, TK)),
                  pl.BlockSpec((TQ, LANES), lambda b, hp, i: (b * nq + i, 0)),
                  pl.BlockSpec((1, N_HEADS, seq), lambda b, hp, i: (b, 0, 0))],
        out_specs=[y_spec, y_spec],
        out_shape=[y_shape, y_shape],
        scratch_shapes=[pair_tile, pltpu.VMEM((HEADS_PER_BLOCK, TQ, seq), F32), pair_tile, pair_tile, pair_tile],
        compiler_params=pltpu.CompilerParams(dimension_semantics=("parallel", "parallel", "arbitrary"),
                                             vmem_limit_bytes=VMEM_LIMIT),
        name="attention",
    )(qkv, qkv, qkv, qkv, qkv, qkv, tri_u, cf_col, cf_row)


def _post_body(x_ref, ya_ref, yb_ref, g_ref, wgate_ref, bgate_ref, wua_ref, wub_ref, wout_ref, o_ref):
    x = x_ref[...]
    h = _rms(x, g_ref[...]).astype(BF16)
    ya = ya_ref[...]
    yb = yb_ref[...]
    acc = jnp.zeros(x.shape, F32)
    nchunk = D_MODEL // WIDTH
    for c in range(nchunk):
        sl_a = slice(c * WIDTH, (c + 1) * WIDTH)
        sl_b = slice(D_MODEL + c * WIDTH, D_MODEL + (c + 1) * WIDTH)
        ga = jax.nn.sigmoid(jnp.dot(h, wgate_ref[:, sl_a].astype(BF16), preferred_element_type=F32)
                            + bgate_ref[:, sl_a])
        gb = jax.nn.sigmoid(jnp.dot(h, wgate_ref[:, sl_b].astype(BF16), preferred_element_type=F32)
                            + bgate_ref[:, sl_b])
        ua = jnp.dot(ya, wua_ref[:, sl_a].astype(BF16), preferred_element_type=F32)
        ub = jnp.dot(yb, wub_ref[:, sl_a].astype(BF16), preferred_element_type=F32)
        mixed = (ga * ua + gb * ub).astype(BF16)
        acc = acc + jnp.dot(mixed, wout_ref[sl_a, :].astype(BF16), preferred_element_type=F32)
    o_ref[...] = x + acc


def _post(x2d, ya, yb, g, wgate, bgate, wua, wub, wout):
    t = x2d.shape[0]
    tile = pl.BlockSpec((TM_FFN, D_MODEL), lambda i: (i, 0))
    ytile = pl.BlockSpec((TM_FFN, WIDTH), lambda i: (i, 0))
    return pl.pallas_call(
        _post_body,
        grid=(t // TM_FFN,),
        in_specs=[tile, ytile, ytile, _const_spec((1, D_MODEL)), _const_spec((D_MODEL, 2 * D_MODEL)),
                  _const_spec((1, 2 * D_MODEL)), _const_spec((WIDTH, D_MODEL)),
                  _const_spec((WIDTH, D_MODEL)), _const_spec((D_MODEL, D_MODEL))],
        out_specs=tile,
        out_shape=jax.ShapeDtypeStruct((t, D_MODEL), F32),
        compiler_params=pltpu.CompilerParams(dimension_semantics=("parallel",),
                                             vmem_limit_bytes=VMEM_LIMIT),
        name="post",
    )(x2d, ya, yb, g, wgate, bgate, wua, wub, wout)


def kernel(x, norm_ffn1, w_ffn1_gate, w_ffn1_up, w_ffn1_down, norm_mix, w_in, b_forget, w_gate, b_gate,
           w_up_a, w_up_b, w_out, norm_ffn2, w_ffn2_gate, w_ffn2_up, w_ffn2_down, norm_final):
    batch, seq, _ = x.shape
    depth = norm_ffn1.shape[0]
    x2d = x.reshape(batch * seq, D_MODEL)
    gf = norm_final.reshape(1, D_MODEL)

    row = lax.broadcasted_iota(jnp.int32, (TM_PROJ, TM_PROJ), 0)
    col = lax.broadcasted_iota(jnp.int32, (TM_PROJ, TM_PROJ), 1)
    tri_l = (col <= row).astype(BF16)
    rk = lax.broadcasted_iota(jnp.int32, (TK, TK), 0)
    ck = lax.broadcasted_iota(jnp.int32, (TK, TK), 1)
    tri_u = jnp.tile((rk >= ck).astype(BF16), (2, 1))

    for l in range(depth):
        last = l == depth - 1
        x2d = _ffn(x2d, norm_ffn1[l].reshape(1, D_MODEL), w_ffn1_gate[l], w_ffn1_up[l], w_ffn1_down[l], gf,
                   final_norm=False)

        n_f = w_in.shape[2] - 6 * WIDTH
        w_f = jnp.pad(w_in[l][:, 6 * WIDTH:], ((0, 0), (0, LANES - n_f))).astype(BF16)
        bf_p = jnp.pad(b_forget[l], (0, LANES - n_f)).reshape(1, LANES)
        g_mix = norm_mix[l].reshape(1, D_MODEL)
        qkv, cf_col, cf_row = _proj(x2d, g_mix, w_in[l], w_f, bf_p, tri_l, batch, seq)
        y_a, y_b = _attention(qkv, tri_u, cf_col, cf_row, batch, seq)
        x2d = _post(x2d, y_a, y_b, g_mix, w_gate[l], b_gate[l].reshape(1, 2 * D_MODEL), w_up_a[l], w_up_b[l], w_out[l])

        x2d = _ffn(x2d, norm_ffn2[l].reshape(1, D_MODEL), w_ffn2_gate[l], w_ffn2_up[l], w_ffn2_down[l], gf,
                   final_norm=last)
    return x2d.reshape(batch, seq, D_MODEL)
```

```python
import functools

import jax
import jax.numpy as jnp
from jax import lax
from jax.experimental import pallas as pl
from jax.experimental.pallas import tpu as pltpu

D_MODEL = 1024
D_FF = 2816
HEAD_DIM = 64
N_HEADS = 8
WIDTH = N_HEADS * HEAD_DIM
RMS_EPS = 1e-6
ATTN_SCALE = HEAD_DIM ** -0.5
LOG2E = 1.4426950408889634

LANES = 128
HEADS_PER_BLOCK = LANES // HEAD_DIM
N_HEAD_BLOCKS = WIDTH // LANES

TM_FFN = 512
TF = 256
TM_PROJ = 512
TK = 256
TQ = 2 * TK
VMEM_LIMIT = 56 * 1024 * 1024

F32 = jnp.float32
BF16 = jnp.bfloat16
NEG_BIG = -1e30
SB_MASS_CUTOFF = 160.0


def _rms(x, g):
    ms = jnp.mean(x * x, axis=-1, keepdims=True)
    return x * lax.rsqrt(ms + RMS_EPS) * g


def _const_spec(shape):
    nd = len(shape)
    return pl.BlockSpec(shape, lambda *_: (0,) * nd, pipeline_mode=pl.Buffered(1))


def _ffn_body(x_ref, g_ref, wg_ref, wu_ref, wd_ref, gf_ref, o_ref, *, final_norm):
    x = x_ref[...]
    xn = _rms(x, g_ref[...]).astype(BF16)
    acc = jnp.zeros(x.shape, F32)
    for c in range(D_FF // TF):
        sl = slice(c * TF, (c + 1) * TF)
        g = jnp.dot(xn, wg_ref[:, sl].astype(BF16), preferred_element_type=F32)
        u = jnp.dot(xn, wu_ref[:, sl].astype(BF16), preferred_element_type=F32)
        h = (g * jax.nn.sigmoid(g) * u).astype(BF16)
        acc = acc + jnp.dot(h, wd_ref[sl, :].astype(BF16), preferred_element_type=F32)
    y = x + 0.5 * acc
    if final_norm:
        y = _rms(y, gf_ref[...])
    o_ref[...] = y


def _ffn(x2d, g, wg, wu, wd, gf, *, final_norm):
    t = x2d.shape[0]
    tile = pl.BlockSpec((TM_FFN, D_MODEL), lambda i: (i, 0))
    return pl.pallas_call(
        functools.partial(_ffn_body, final_norm=final_norm),
        grid=(t // TM_FFN,),
        in_specs=[tile, _const_spec((1, D_MODEL)), _const_spec((D_MODEL, D_FF)),
                  _const_spec((D_MODEL, D_FF)), _const_spec((D_FF, D_MODEL)),
                  _const_spec((1, D_MODEL))],
        out_specs=tile,
        out_shape=jax.ShapeDtypeStruct((t, D_MODEL), F32),
        compiler_params=pltpu.CompilerParams(dimension_semantics=("parallel",),
                                             vmem_limit_bytes=VMEM_LIMIT),
        name="ffn_final" if final_norm else "ffn",
    )(x2d, g, wg, wu, wd, gf)


def _proj_body(x_ref, g_ref, w_ref, wf_ref, bf_ref, tri_ref, qkv_ref, cfc_ref, cfr_ref, carry_ref):
    @pl.when(pl.program_id(1) == 0)
    def _():
        carry_ref[...] = jnp.zeros_like(carry_ref)

    h = _rms(x_ref[...], g_ref[...]).astype(BF16)

    def qkv_chunk(c):
        sl = slice(c * WIDTH, (c + 1) * WIDTH)
        p = jnp.dot(h, w_ref[:, sl].astype(BF16), preferred_element_type=F32)
        if c in (0, 3):
            p = p * (ATTN_SCALE * LOG2E)
        qkv_ref[:, sl] = p.astype(BF16)

    fl = jnp.dot(h, wf_ref[...], preferred_element_type=F32) + bf_ref[...]
    lf = (jnp.minimum(fl, 0.0) - jnp.log1p(jnp.exp(-jnp.abs(fl)))) * LOG2E
    hi = lf.astype(BF16)
    r1 = lf - hi.astype(F32)
    mid = r1.astype(BF16)
    lo = (r1 - mid.astype(F32)).astype(BF16)
    for c in range(3):
        qkv_chunk(c)
    cs = jnp.dot(tri_ref[...], jnp.concatenate([hi, mid, lo], axis=1), preferred_element_type=F32)
    for c in range(3, 6):
        qkv_chunk(c)
    cf = cs[:, :LANES] + cs[:, LANES:2 * LANES] + cs[:, 2 * LANES:] + carry_ref[...]
    carry_ref[...] = cf[TM_PROJ - 1:TM_PROJ, :]
    cfc_ref[...] = cf
    cfr_ref[0] = cf.T[:N_HEADS, :]


def _proj(x2d, g, w, wf, bf, tri, batch, seq):
    t = x2d.shape[0]
    ns = seq // TM_PROJ
    return pl.pallas_call(
        _proj_body,
        grid=(batch, ns),
        in_specs=[pl.BlockSpec((TM_PROJ, D_MODEL), lambda b, s: (b * ns + s, 0)),
                  _const_spec((1, D_MODEL)), _const_spec(w.shape), _const_spec((D_MODEL, LANES)),
                  _const_spec((1, LANES)), _const_spec((TM_PROJ, TM_PROJ))],
        out_specs=[pl.BlockSpec((TM_PROJ, 6 * WIDTH), lambda b, s: (b * ns + s, 0)),
                   pl.BlockSpec((TM_PROJ, LANES), lambda b, s: (b * ns + s, 0)),
                   pl.BlockSpec((1, N_HEADS, TM_PROJ), lambda b, s: (b, 0, s))],
        out_shape=[jax.ShapeDtypeStruct((t, 6 * WIDTH), BF16),
                   jax.ShapeDtypeStruct((t, LANES), F32),
                   jax.ShapeDtypeStruct((batch, N_HEADS, seq), F32)],
        scratch_shapes=[pltpu.VMEM((1, LANES), F32)],
        compiler_params=pltpu.CompilerParams(dimension_semantics=("arbitrary", "arbitrary"),
                                             vmem_limit_bytes=VMEM_LIMIT),
        name="proj",
    )(x2d, g, w, wf, bf, tri)


def _head_split(q):
    lane = lax.broadcasted_iota(jnp.int32, q.shape, 1)
    zero = jnp.zeros_like(q)
    return [jnp.where(lane < HEAD_DIM, q, zero), jnp.where(lane >= HEAD_DIM, q, zero)]


def _qk(qh, kj):
    return lax.dot_general(qh, kj, (((1,), (1,)), ((), ())), preferred_element_type=F32)


def _lane_chunks(x):
    return [x[:, c * LANES:(c + 1) * LANES] for c in range(x.shape[1] // LANES)]


def _attn_qblock(i, qa_ref, ka_ref, va_ref, qb_ref, kb_ref, vb_ref, u_ref, cfc_ref, cfr_ref, ya_ref, yb_ref,
                 acc_a, s_ref, m_ref, l_ref, acc_b):
    hp = pl.program_id(1)
    heads = range(HEADS_PER_BLOCK)
    lo_rows, hi_rows, all_rows = slice(0, TK), slice(TK, TQ), slice(0, TQ)
    t_idx = lax.broadcasted_iota(jnp.int32, (TK, TK), 0)
    s_idx = lax.broadcasted_iota(jnp.int32, (TK, TK), 1)
    strict = s_idx < t_idx
    causal = s_idx <= t_idx
    lane = lax.broadcasted_iota(jnp.int32, (TQ, LANES), 1)
    diag_off = pl.multiple_of(i * TQ, TQ)

    qa = _head_split(qa_ref[pl.ds(diag_off, TQ), :])

    def key_off(j):
        return pl.multiple_of(j * TK, TK)

    def sb_qk(tasks):
        return [_qk(qa[h][rows], ka_ref[pl.ds(key_off(j), TK), :]) for j, h, rows, _ in tasks]

    def sb_cumsum(tasks, zs):
        cs = []
        for z, (_, _, _, diag) in zip(zs, tasks):
            sp = jnp.maximum(z, 0.0) + jnp.log2(1.0 + jnp.exp2(jnp.minimum(z, -z)))
            if diag:
                sp = jnp.where(strict, sp, 0.0)
            hi = sp.astype(BF16)
            lo = (sp - hi.astype(F32)).astype(BF16)
            cs.append(jnp.dot(jnp.concatenate([hi, lo], axis=1), u_ref[...], preferred_element_type=F32))
        return cs

    def sb_pv(tasks, zs, cs):
        outs = []
        for z, c, (j, _, _, diag) in zip(zs, cs, tasks):
            w = jnp.exp2(z - c)
            if diag:
                w = jnp.where(strict, w, 0.0)
            o = jnp.dot(w.astype(BF16), va_ref[pl.ds(key_off(j), TK), :], preferred_element_type=F32)
            outs.append((o, c[:, 0:1]))
        return outs

    def sb_block_tasks(j):
        return [(j, h, rows, False) for h in heads for rows in (lo_rows, hi_rows)]

    def sb_add_block(res, mass, gate=None):
        new_mass = []
        for h in heads:
            (o_l, t_l), (o_h, t_h) = res[2 * h:2 * h + 2]
            o, t = jnp.concatenate([o_l, o_h], axis=0), jnp.concatenate([t_l, t_h], axis=0)
            scale = jnp.exp2(-mass[h])
            if gate is not None:
                scale, t = scale * gate, t * gate
            acc_a[h] += o * scale
            new_mass.append(mass[h] + t)
        return new_mass

    qb = _head_split(qb_ref[pl.ds(diag_off, TQ), :])
    cfc = cfc_ref[pl.ds(diag_off, TQ), :]
    fox_heads = [hp * HEADS_PER_BLOCK + h for h in heads]
    cf_t = [jnp.broadcast_to(jnp.sum(jnp.where(lane == hd, cfc, 0.0), axis=1, keepdims=True), (TQ, LANES))
            for hd in fox_heads]

    def fox_qk(off, width, h, rows):
        return _qk(qb[h][rows], kb_ref[pl.ds(off, width), :])

    def fox_logits(z, off, width, h, rows):
        cf_s = cfr_ref[0, pl.ds(fox_heads[h], 1), pl.ds(off, width)]
        return jnp.concatenate([zc + cf_t[h][rows] for zc in _lane_chunks(z)], axis=1) - cf_s

    def chunk_max(s):
        return functools.reduce(jnp.maximum, _lane_chunks(s))

    def fox_probs(off, width, h, rows):
        s = s_ref[h, rows, pl.ds(off, width)]
        m = m_ref[h, rows, :]
        return jnp.concatenate([jnp.exp2(sc - m) for sc in _lane_chunks(s)], axis=1).astype(BF16)

    def fox_pv(p, off, width):
        v1 = jnp.concatenate([vb_ref[pl.ds(off, width), :], jnp.ones((width, LANES), BF16)], axis=1)
        o = jnp.dot(p, v1, preferred_element_type=F32)
        return o[:, :LANES], o[:, LANES:]

    diag_tasks = [t for h in heads for t in ((2 * i + 1, h, hi_rows, True), (2 * i, h, hi_rows, False),
                                             (2 * i, h, lo_rows, True))]
    zs = sb_qk(diag_tasks)
    fz = [(fox_qk(diag_off, TQ, h, hi_rows), fox_qk(diag_off, TK, h, lo_rows)) for h in heads]
    cs = sb_cumsum(diag_tasks, zs)
    for h in heads:
        s_hi = fox_logits(fz[h][0], diag_off, TQ, h, hi_rows)
        s_hi = jnp.concatenate([s_hi[:, :TK], jnp.where(causal, s_hi[:, TK:], NEG_BIG)], axis=1)
        s_lo = jnp.where(causal, fox_logits(fz[h][1], diag_off, TK, h, lo_rows), NEG_BIG)
        s_ref[h, hi_rows, pl.ds(diag_off, TQ)] = s_hi
        s_ref[h, lo_rows, pl.ds(diag_off, TK)] = s_lo
        m_ref[h, hi_rows, :] = chunk_max(s_hi)
        m_ref[h, lo_rows, :] = chunk_max(s_lo)
    res = sb_pv(diag_tasks, zs, cs)
    mass = []
    for h in heads:
        (o_b, r_b), (o_a, r_a), (o_l, r_l) = res[3 * h:3 * h + 3]
        acc_a[h, hi_rows, :] = o_b + o_a * jnp.exp2(-r_b)
        acc_a[h, lo_rows, :] = o_l
        mass.append(jnp.concatenate([r_l, r_a + r_b], axis=0))

    def pass1(jj, _):
        off = pl.multiple_of(jj * TQ, TQ)
        for h in heads:
            s = fox_logits(fox_qk(off, TQ, h, all_rows), off, TQ, h, all_rows)
            s_ref[h, :, pl.ds(off, TQ)] = s
            m_ref[h] = jnp.maximum(m_ref[h], chunk_max(s))
        return 0

    lax.fori_loop(0, i, pass1, 0)

    for h in heads:
        m_ref[h] = jnp.broadcast_to(jnp.max(m_ref[h], axis=1, keepdims=True), (TQ, LANES))
    first_tasks = sb_block_tasks(jnp.maximum(2 * i - 1, 0))
    zs = sb_qk(first_tasks)
    probs = [(fox_probs(diag_off, TQ, h, hi_rows), fox_probs(diag_off, TK, h, lo_rows)) for h in heads]
    for h in heads:
        acc_b[h, hi_rows, :], l_ref[h, hi_rows, :] = fox_pv(probs[h][0], diag_off, TQ)
        acc_b[h, lo_rows, :], l_ref[h, lo_rows, :] = fox_pv(probs[h][1], diag_off, TK)
    cs = sb_cumsum(first_tasks, zs)
    res = sb_pv(first_tasks, zs, cs)
    mass = sb_add_block(res, mass, gate=(i > 0).astype(F32))

    def pass2(jj, _):
        off = pl.multiple_of(jj * TQ, TQ)
        for h in heads:
            o, l = fox_pv(fox_probs(off, TQ, h, all_rows), off, TQ)
            acc_b[h] += o
            l_ref[h] += l
        return 0

    lax.fori_loop(0, i, pass2, 0)

    def min_mass(mass):
        return jnp.min(functools.reduce(jnp.minimum, mass))

    def live(carry):
        return jnp.logical_and(carry[0] >= 0, carry[1] < SB_MASS_CUTOFF)

    def step(carry):
        j, mass = carry[0], carry[2:]
        tasks = sb_block_tasks(j)
        zs = sb_qk(tasks)
        new_mass = sb_add_block(sb_pv(tasks, zs, sb_cumsum(tasks, zs)), mass)
        return (j - 1, min_mass(new_mass), *new_mass)

    lax.while_loop(live, step, (2 * i - 2, min_mass(mass), *mass))

    ya_ref[pl.ds(diag_off, TQ), :] = jnp.where(lane < HEAD_DIM, acc_a[0], acc_a[1]).astype(BF16)
    yb = [acc_b[h] / l_ref[h] for h in heads]
    yb_ref[pl.ds(diag_off, TQ), :] = jnp.where(lane < HEAD_DIM, yb[0], yb[1]).astype(BF16)


def _attn_body(*refs, nq):
    def qblock(i, carry):
        _attn_qblock(i, *refs)
        return carry

    lax.fori_loop(0, nq, qblock, 0)


def _attention(qkv, tri_u, cf_col, cf_row, batch, seq):
    nb = N_HEAD_BLOCKS

    def seq_spec(col):
        return pl.BlockSpec((seq, LANES), lambda b, hp: (b, col + hp))

    y_spec = seq_spec(0)
    y_shape = jax.ShapeDtypeStruct((batch * seq, WIDTH), BF16)
    pair_tile = pltpu.VMEM((HEADS_PER_BLOCK, TQ, LANES), F32)
    return pl.pallas_call(
        functools.partial(_attn_body, nq=seq // TQ),
        grid=(batch, nb),
        in_specs=[seq_spec(c * nb) for c in range(6)]
        + [_const_spec((2 * TK, TK)),
           pl.BlockSpec((seq, LANES), lambda b, hp: (b, 0)),
           pl.BlockSpec((1, N_HEADS, seq), lambda b, hp: (b, 0, 0))],
        out_specs=[y_spec, y_spec],
        out_shape=[y_shape, y_shape],
        scratch_shapes=[pair_tile, pltpu.VMEM((HEADS_PER_BLOCK, TQ, seq), F32), pair_tile, pair_tile, pair_tile],
        compiler_params=pltpu.CompilerParams(dimension_semantics=("parallel", "parallel"),
                                             vmem_limit_bytes=VMEM_LIMIT),
        name="attention",
    )(qkv, qkv, qkv, qkv, qkv, qkv, tri_u, cf_col, cf_row)


def _post_body(x_ref, ya_ref, yb_ref, g_ref, wgate_ref, bgate_ref, wua_ref, wub_ref, wout_ref, o_ref):
    x = x_ref[...]
    h = _rms(x, g_ref[...]).astype(BF16)
    ya = ya_ref[...]
    yb = yb_ref[...]
    acc = jnp.zeros(x.shape, F32)
    nchunk = D_MODEL // WIDTH
    for c in range(nchunk):
        sl_a = slice(c * WIDTH, (c + 1) * WIDTH)
        sl_b = slice(D_MODEL + c * WIDTH, D_MODEL + (c + 1) * WIDTH)
        ga = jax.nn.sigmoid(jnp.dot(h, wgate_ref[:, sl_a].astype(BF16), preferred_element_type=F32)
                            + bgate_ref[:, sl_a])
        gb = jax.nn.sigmoid(jnp.dot(h, wgate_ref[:, sl_b].astype(BF16), preferred_element_type=F32)
                            + bgate_ref[:, sl_b])
        ua = jnp.dot(ya, wua_ref[:, sl_a].astype(BF16), preferred_element_type=F32)
        ub = jnp.dot(yb, wub_ref[:, sl_a].astype(BF16), preferred_element_type=F32)
        mixed = (ga * ua + gb * ub).astype(BF16)
        acc = acc + jnp.dot(mixed, wout_ref[sl_a, :].astype(BF16), preferred_element_type=F32)
    o_ref[...] = x + acc


def _post(x2d, ya, yb, g, wgate, bgate, wua, wub, wout):
    t = x2d.shape[0]
    tile = pl.BlockSpec((TM_FFN, D_MODEL), lambda i: (i, 0))
    ytile = pl.BlockSpec((TM_FFN, WIDTH), lambda i: (i, 0))
    return pl.pallas_call(
        _post_body,
        grid=(t // TM_FFN,),
        in_specs=[tile, ytile, ytile, _const_spec((1, D_MODEL)), _const_spec((D_MODEL, 2 * D_MODEL)),
                  _const_spec((1, 2 * D_MODEL)), _const_spec((WIDTH, D_MODEL)),
                  _const_spec((WIDTH, D_MODEL)), _const_spec((D_MODEL, D_MODEL))],
        out_specs=tile,
        out_shape=jax.ShapeDtypeStruct((t, D_MODEL), F32),
        compiler_params=pltpu.CompilerParams(dimension_semantics=("parallel",),
                                             vmem_limit_bytes=VMEM_LIMIT),
        name="post",
    )(x2d, ya, yb, g, wgate, bgate, wua, wub, wout)


def kernel(x, norm_ffn1, w_ffn1_gate, w_ffn1_up, w_ffn1_down, norm_mix, w_in, b_forget, w_gate, b_gate,
           w_up_a, w_up_b, w_out, norm_ffn2, w_ffn2_gate, w_ffn2_up, w_ffn2_down, norm_final):
    batch, seq, _ = x.shape
    depth = norm_ffn1.shape[0]
    x2d = x.reshape(batch * seq, D_MODEL)
    gf = norm_final.reshape(1, D_MODEL)

    row = lax.broadcasted_iota(jnp.int32, (TM_PROJ, TM_PROJ), 0)
    col = lax.broadcasted_iota(jnp.int32, (TM_PROJ, TM_PROJ), 1)
    tri_l = (col <= row).astype(BF16)
    rk = lax.broadcasted_iota(jnp.int32, (TK, TK), 0)
    ck = lax.broadcasted_iota(jnp.int32, (TK, TK), 1)
    tri_u = jnp.tile((rk >= ck).astype(BF16), (2, 1))

    for l in range(depth):
        last = l == depth - 1
        x2d = _ffn(x2d, norm_ffn1[l].reshape(1, D_MODEL), w_ffn1_gate[l], w_ffn1_up[l], w_ffn1_down[l], gf,
                   final_norm=False)

        n_f = w_in.shape[2] - 6 * WIDTH
        w_f = jnp.pad(w_in[l][:, 6 * WIDTH:], ((0, 0), (0, LANES - n_f))).astype(BF16)
        bf_p = jnp.pad(b_forget[l], (0, LANES - n_f)).reshape(1, LANES)
        g_mix = norm_mix[l].reshape(1, D_MODEL)
        qkv, cf_col, cf_row = _proj(x2d, g_mix, w_in[l], w_f, bf_p, tri_l, batch, seq)
        y_a, y_b = _attention(qkv, tri_u, cf_col, cf_row, batch, seq)
        x2d = _post(x2d, y_a, y_b, g_mix, w_gate[l], b_gate[l].reshape(1, 2 * D_MODEL), w_up_a[l], w_up_b[l], w_out[l])

        x2d = _ffn(x2d, norm_ffn2[l].reshape(1, D_MODEL), w_ffn2_gate[l], w_ffn2_up[l], w_ffn2_down[l], gf,
                   final_norm=last)
    return x2d.reshape(batch, seq, D_MODEL)
```

```python
import functools

import jax
import jax.numpy as jnp
from jax import lax
from jax.experimental import pallas as pl
from jax.experimental.pallas import tpu as pltpu

D_MODEL = 1024
D_FF = 2816
HEAD_DIM = 64
N_HEADS = 8
WIDTH = N_HEADS * HEAD_DIM
RMS_EPS = 1e-6
ATTN_SCALE = HEAD_DIM ** -0.5
LOG2E = 1.4426950408889634

LANES = 128
HEADS_PER_BLOCK = LANES // HEAD_DIM
N_HEAD_BLOCKS = WIDTH // LANES

TM_FFN = 512
TF = 256
TM_PROJ = 512
TK = 256
TQ = 2 * TK
VMEM_LIMIT = 56 * 1024 * 1024

F32 = jnp.float32
BF16 = jnp.bfloat16
NEG_BIG = -1e30
SB_MASS_CUTOFF = 160.0


def _rms(x, g):
    ms = jnp.mean(x * x, axis=-1, keepdims=True)
    return x * lax.rsqrt(ms + RMS_EPS) * g


def _const_spec(shape):
    nd = len(shape)
    return pl.BlockSpec(shape, lambda *_: (0,) * nd, pipeline_mode=pl.Buffered(1))


def _ffn_body(x_ref, g_ref, wg_ref, wu_ref, wd_ref, gf_ref, o_ref, *, final_norm):
    x = x_ref[...]
    xn = _rms(x, g_ref[...]).astype(BF16)
    acc = jnp.zeros(x.shape, F32)
    for c in range(D_FF // TF):
        sl = slice(c * TF, (c + 1) * TF)
        g = jnp.dot(xn, wg_ref[:, sl].astype(BF16), preferred_element_type=F32)
        u = jnp.dot(xn, wu_ref[:, sl].astype(BF16), preferred_element_type=F32)
        h = (g * jax.nn.sigmoid(g) * u).astype(BF16)
        acc = acc + jnp.dot(h, wd_ref[sl, :].astype(BF16), preferred_element_type=F32)
    y = x + 0.5 * acc
    if final_norm:
        y = _rms(y, gf_ref[...])
    o_ref[...] = y


def _ffn(x2d, g, wg, wu, wd, gf, *, final_norm):
    t = x2d.shape[0]
    tile = pl.BlockSpec((TM_FFN, D_MODEL), lambda i: (i, 0))
    return pl.pallas_call(
        functools.partial(_ffn_body, final_norm=final_norm),
        grid=(t // TM_FFN,),
        in_specs=[tile, _const_spec((1, D_MODEL)), _const_spec((D_MODEL, D_FF)),
                  _const_spec((D_MODEL, D_FF)), _const_spec((D_FF, D_MODEL)),
                  _const_spec((1, D_MODEL))],
        out_specs=tile,
        out_shape=jax.ShapeDtypeStruct((t, D_MODEL), F32),
        compiler_params=pltpu.CompilerParams(dimension_semantics=("parallel",),
                                             vmem_limit_bytes=VMEM_LIMIT),
        name="ffn_final" if final_norm else "ffn",
    )(x2d, g, wg, wu, wd, gf)


def _proj_body(x_ref, g_ref, w_ref, wf_ref, bf_ref, tri_ref, qkv_ref, cfc_ref, cfr_ref, carry_ref):
    @pl.when(pl.program_id(1) == 0)
    def _():
        carry_ref[...] = jnp.zeros_like(carry_ref)

    h = _rms(x_ref[...], g_ref[...]).astype(BF16)

    def qkv_chunk(c):
        sl = slice(c * WIDTH, (c + 1) * WIDTH)
        p = jnp.dot(h, w_ref[:, sl].astype(BF16), preferred_element_type=F32)
        if c in (0, 3):
            p = p * (ATTN_SCALE * LOG2E)
        qkv_ref[:, sl] = p.astype(BF16)

    fl = jnp.dot(h, wf_ref[...], preferred_element_type=F32) + bf_ref[...]
    lf = (jnp.minimum(fl, 0.0) - jnp.log1p(jnp.exp(-jnp.abs(fl)))) * LOG2E
    hi = lf.astype(BF16)
    r1 = lf - hi.astype(F32)
    mid = r1.astype(BF16)
    lo = (r1 - mid.astype(F32)).astype(BF16)
    for c in range(3):
        qkv_chunk(c)
    cs = jnp.dot(tri_ref[...], jnp.concatenate([hi, mid, lo], axis=1), preferred_element_type=F32)
    for c in range(3, 6):
        qkv_chunk(c)
    cf = cs[:, :LANES] + cs[:, LANES:2 * LANES] + cs[:, 2 * LANES:] + carry_ref[...]
    carry_ref[...] = cf[TM_PROJ - 1:TM_PROJ, :]
    cfc_ref[...] = cf
    cfr_ref[0] = cf.T[:N_HEADS, :]


def _proj(x2d, g, w, wf, bf, tri, batch, seq):
    t = x2d.shape[0]
    ns = seq // TM_PROJ
    return pl.pallas_call(
        _proj_body,
        grid=(batch, ns),
        in_specs=[pl.BlockSpec((TM_PROJ, D_MODEL), lambda b, s: (b * ns + s, 0)),
                  _const_spec((1, D_MODEL)), _const_spec(w.shape), _const_spec((D_MODEL, LANES)),
                  _const_spec((1, LANES)), _const_spec((TM_PROJ, TM_PROJ))],
        out_specs=[pl.BlockSpec((TM_PROJ, 6 * WIDTH), lambda b, s: (b * ns + s, 0)),
                   pl.BlockSpec((TM_PROJ, LANES), lambda b, s: (b * ns + s, 0)),
                   pl.BlockSpec((1, N_HEADS, TM_PROJ), lambda b, s: (b, 0, s))],
        out_shape=[jax.ShapeDtypeStruct((t, 6 * WIDTH), BF16),
                   jax.ShapeDtypeStruct((t, LANES), F32),
                   jax.ShapeDtypeStruct((batch, N_HEADS, seq), F32)],
        scratch_shapes=[pltpu.VMEM((1, LANES), F32)],
        compiler_params=pltpu.CompilerParams(dimension_semantics=("arbitrary", "arbitrary"),
                                             vmem_limit_bytes=VMEM_LIMIT),
        name="proj",
    )(x2d, g, w, wf, bf, tri)


def _head_split(q):
    lane = lax.broadcasted_iota(jnp.int32, q.shape, 1)
    zero = jnp.zeros_like(q)
    return [jnp.where(lane < HEAD_DIM, q, zero), jnp.where(lane >= HEAD_DIM, q, zero)]


def _qk(qh, kj):
    return lax.dot_general(qh, kj, (((1,), (1,)), ((), ())), preferred_element_type=F32)


def _lane_chunks(x):
    return [x[:, c * LANES:(c + 1) * LANES] for c in range(x.shape[1] // LANES)]


def _attn_qblock(i, qa_ref, ka_ref, va_ref, qb_ref, kb_ref, vb_ref, u_ref, cfc_ref, cfr_ref, ya_ref, yb_ref,
                 acc_a, s_ref, m_ref, l_ref, acc_b):
    hp = pl.program_id(1)
    heads = range(HEADS_PER_BLOCK)
    lo_rows, hi_rows, all_rows = slice(0, TK), slice(TK, TQ), slice(0, TQ)
    t_idx = lax.broadcasted_iota(jnp.int32, (TK, TK), 0)
    s_idx = lax.broadcasted_iota(jnp.int32, (TK, TK), 1)
    strict = s_idx < t_idx
    causal = s_idx <= t_idx
    lane = lax.broadcasted_iota(jnp.int32, (TQ, LANES), 1)
    diag_off = pl.multiple_of(i * TQ, TQ)

    qa = _head_split(qa_ref[pl.ds(diag_off, TQ), :])

    def key_off(j):
        return pl.multiple_of(j * TK, TK)

    def sb_qk(tasks):
        return [_qk(qa[h][rows], ka_ref[pl.ds(key_off(j), TK), :]) for j, h, rows, _ in tasks]

    def sb_cumsum(tasks, zs):
        cs = []
        for z, (_, _, _, diag) in zip(zs, tasks):
            sp = jnp.maximum(z, 0.0) + jnp.log2(1.0 + jnp.exp2(jnp.minimum(z, -z)))
            if diag:
                sp = jnp.where(strict, sp, 0.0)
            cs.append(jnp.dot(sp.astype(BF16), u_ref[...], preferred_element_type=F32))
        return cs

    def sb_pv(tasks, zs, cs):
        outs = []
        for z, c, (j, _, _, diag) in zip(zs, cs, tasks):
            w = jnp.exp2(z - c)
            if diag:
                w = jnp.where(strict, w, 0.0)
            o = jnp.dot(w.astype(BF16), va_ref[pl.ds(key_off(j), TK), :], preferred_element_type=F32)
            outs.append((o, c[:, 0:1]))
        return outs

    def sb_block_tasks(j):
        return [(j, h, rows, False) for h in heads for rows in (lo_rows, hi_rows)]

    def sb_add_block(res, mass, gate=None):
        new_mass = []
        for h in heads:
            (o_l, t_l), (o_h, t_h) = res[2 * h:2 * h + 2]
            o, t = jnp.concatenate([o_l, o_h], axis=0), jnp.concatenate([t_l, t_h], axis=0)
            scale = jnp.exp2(-mass[h])
            if gate is not None:
                scale, t = scale * gate, t * gate
            acc_a[h] += o * scale
            new_mass.append(mass[h] + t)
        return new_mass

    qb = _head_split(qb_ref[pl.ds(diag_off, TQ), :])
    cfc = cfc_ref[pl.ds(diag_off, TQ), :]
    fox_heads = [hp * HEADS_PER_BLOCK + h for h in heads]
    cf_t = [jnp.broadcast_to(jnp.sum(jnp.where(lane == hd, cfc, 0.0), axis=1, keepdims=True), (TQ, LANES))
            for hd in fox_heads]

    def fox_qk(off, width, h, rows):
        return _qk(qb[h][rows], kb_ref[pl.ds(off, width), :])

    def fox_logits(z, off, width, h, rows):
        cf_s = cfr_ref[0, pl.ds(fox_heads[h], 1), pl.ds(off, width)]
        return jnp.concatenate([zc + cf_t[h][rows] for zc in _lane_chunks(z)], axis=1) - cf_s

    def chunk_max(s):
        return functools.reduce(jnp.maximum, _lane_chunks(s))

    def fox_probs(off, width, h, rows):
        s = s_ref[h, rows, pl.ds(off, width)]
        m = m_ref[h, rows, :]
        return jnp.concatenate([jnp.exp2(sc - m) for sc in _lane_chunks(s)], axis=1).astype(BF16)

    def fox_pv(p, off, width):
        v1 = jnp.concatenate([vb_ref[pl.ds(off, width), :], jnp.ones((width, LANES), BF16)], axis=1)
        o = jnp.dot(p, v1, preferred_element_type=F32)
        return o[:, :LANES], o[:, LANES:]

    diag_tasks = [t for h in heads for t in ((2 * i + 1, h, hi_rows, True), (2 * i, h, hi_rows, False),
                                             (2 * i, h, lo_rows, True))]
    zs = sb_qk(diag_tasks)
    fz = [(fox_qk(diag_off, TQ, h, hi_rows), fox_qk(diag_off, TK, h, lo_rows)) for h in heads]
    cs = sb_cumsum(diag_tasks, zs)
    for h in heads:
        s_hi = fox_logits(fz[h][0], diag_off, TQ, h, hi_rows)
        s_hi = jnp.concatenate([s_hi[:, :TK], jnp.where(causal, s_hi[:, TK:], NEG_BIG)], axis=1)
        s_lo = jnp.where(causal, fox_logits(fz[h][1], diag_off, TK, h, lo_rows), NEG_BIG)
        s_ref[h, hi_rows, pl.ds(diag_off, TQ)] = s_hi
        s_ref[h, lo_rows, pl.ds(diag_off, TK)] = s_lo
        m_ref[h, hi_rows, :] = chunk_max(s_hi)
        m_ref[h, lo_rows, :] = chunk_max(s_lo)
    res = sb_pv(diag_tasks, zs, cs)
    mass = []
    for h in heads:
        (o_b, r_b), (o_a, r_a), (o_l, r_l) = res[3 * h:3 * h + 3]
        acc_a[h, hi_rows, :] = o_b + o_a * jnp.exp2(-r_b)
        acc_a[h, lo_rows, :] = o_l
        mass.append(jnp.concatenate([r_l, r_a + r_b], axis=0))

    def pass1(jj, _):
        off = pl.multiple_of(jj * TQ, TQ)
        for h in heads:
            s = fox_logits(fox_qk(off, TQ, h, all_rows), off, TQ, h, all_rows)
            s_ref[h, :, pl.ds(off, TQ)] = s
            m_ref[h] = jnp.maximum(m_ref[h], chunk_max(s))
        return 0

    lax.fori_loop(0, i, pass1, 0)

    for h in heads:
        m_ref[h] = jnp.broadcast_to(jnp.max(m_ref[h], axis=1, keepdims=True), (TQ, LANES))
    first_tasks = sb_block_tasks(jnp.maximum(2 * i - 1, 0))
    zs = sb_qk(first_tasks)
    probs = [(fox_probs(diag_off, TQ, h, hi_rows), fox_probs(diag_off, TK, h, lo_rows)) for h in heads]
    for h in heads:
        acc_b[h, hi_rows, :], l_ref[h, hi_rows, :] = fox_pv(probs[h][0], diag_off, TQ)
        acc_b[h, lo_rows, :], l_ref[h, lo_rows, :] = fox_pv(probs[h][1], diag_off, TK)
    cs = sb_cumsum(first_tasks, zs)
    res = sb_pv(first_tasks, zs, cs)
    mass = sb_add_block(res, mass, gate=jnp.where(i > 0, 1.0, 0.0).astype(F32))

    def pass2(jj, _):
        off = pl.multiple_of(jj * TQ, TQ)
        for h in heads:
            o, l = fox_pv(fox_probs(off, TQ, h, all_rows), off, TQ)
            acc_b[h] += o
            l_ref[h] += l
        return 0

    lax.fori_loop(0, i, pass2, 0)

    def min_mass(mass):
        return jnp.min(functools.reduce(jnp.minimum, mass))

    def live(carry):
        return jnp.logical_and(carry[0] >= 0, carry[1] < SB_MASS_CUTOFF)

    def step(carry):
        j, mass = carry[0], carry[2:]
        tasks = sb_block_tasks(j)
        zs = sb_qk(tasks)
        new_mass = sb_add_block(sb_pv(tasks, zs, sb_cumsum(tasks, zs)), mass)
        return (j - 1, min_mass(new_mass), *new_mass)

    lax.while_loop(live, step, (2 * i - 2, min_mass(mass), *mass))

    ya_ref[pl.ds(diag_off, TQ), :] = jnp.where(lane < HEAD_DIM, acc_a[0], acc_a[1]).astype(BF16)
    yb = [acc_b[h] / l_ref[h] for h in heads]
    yb_ref[pl.ds(diag_off, TQ), :] = jnp.where(lane < HEAD_DIM, yb[0], yb[1]).astype(BF16)


def _attn_body(*refs, nq):
    def qblock(i, carry):
        _attn_qblock(i, *refs)
        return carry

    lax.fori_loop(0, nq, qblock, 0)


def _attention(qkv, tri_u, cf_col, cf_row, batch, seq):
    nb = N_HEAD_BLOCKS

    def seq_spec(col):
        return pl.BlockSpec((seq, LANES), lambda b, hp: (b, col + hp))

    y_spec = seq_spec(0)
    y_shape = jax.ShapeDtypeStruct((batch * seq, WIDTH), BF16)
    pair_tile = pltpu.VMEM((HEADS_PER_BLOCK, TQ, LANES), F32)
    return pl.pallas_call(
        functools.partial(_attn_body, nq=seq // TQ),
        grid=(batch, nb),
        in_specs=[seq_spec(c * nb) for c in range(6)]
        + [_const_spec((TK, TK)),
           pl.BlockSpec((seq, LANES), lambda b, hp: (b, 0)),
           pl.BlockSpec((1, N_HEADS, seq), lambda b, hp: (b, 0, 0))],
        out_specs=[y_spec, y_spec],
        out_shape=[y_shape, y_shape],
        scratch_shapes=[pair_tile, pltpu.VMEM((HEADS_PER_BLOCK, TQ, seq), F32), pair_tile, pair_tile, pair_tile],
        compiler_params=pltpu.CompilerParams(dimension_semantics=("parallel", "parallel"),
                                             vmem_limit_bytes=VMEM_LIMIT),
        name="attention",
    )(qkv, qkv, qkv, qkv, qkv, qkv, tri_u, cf_col, cf_row)


def _post_body(x_ref, ya_ref, yb_ref, g_ref, wgate_ref, bgate_ref, wua_ref, wub_ref, wout_ref, o_ref):
    x = x_ref[...]
    h = _rms(x, g_ref[...]).astype(BF16)
    ya = ya_ref[...]
    yb = yb_ref[...]
    acc = jnp.zeros(x.shape, F32)
    nchunk = D_MODEL // WIDTH
    for c in range(nchunk):
        sl_a = slice(c * WIDTH, (c + 1) * WIDTH)
        sl_b = slice(D_MODEL + c * WIDTH, D_MODEL + (c + 1) * WIDTH)
        ga = jax.nn.sigmoid(jnp.dot(h, wgate_ref[:, sl_a].astype(BF16), preferred_element_type=F32)
                            + bgate_ref[:, sl_a])
        gb = jax.nn.sigmoid(jnp.dot(h, wgate_ref[:, sl_b].astype(BF16), preferred_element_type=F32)
                            + bgate_ref[:, sl_b])
        ua = jnp.dot(ya, wua_ref[:, sl_a].astype(BF16), preferred_element_type=F32)
        ub = jnp.dot(yb, wub_ref[:, sl_a].astype(BF16), preferred_element_type=F32)
        mixed = (ga * ua + gb * ub).astype(BF16)
        acc = acc + jnp.dot(mixed, wout_ref[sl_a, :].astype(BF16), preferred_element_type=F32)
    o_ref[...] = x + acc


def _post(x2d, ya, yb, g, wgate, bgate, wua, wub, wout):
    t = x2d.shape[0]
    tile = pl.BlockSpec((TM_FFN, D_MODEL), lambda i: (i, 0))
    ytile = pl.BlockSpec((TM_FFN, WIDTH), lambda i: (i, 0))
    return pl.pallas_call(
        _post_body,
        grid=(t // TM_FFN,),
        in_specs=[tile, ytile, ytile, _const_spec((1, D_MODEL)), _const_spec((D_MODEL, 2 * D_MODEL)),
                  _const_spec((1, 2 * D_MODEL)), _const_spec((WIDTH, D_MODEL)),
                  _const_spec((WIDTH, D_MODEL)), _const_spec((D_MODEL, D_MODEL))],
        out_specs=tile,
        out_shape=jax.ShapeDtypeStruct((t, D_MODEL), F32),
        compiler_params=pltpu.CompilerParams(dimension_semantics=("parallel",),
                                             vmem_limit_bytes=VMEM_LIMIT),
        name="post",
    )(x2d, ya, yb, g, wgate, bgate, wua, wub, wout)


def kernel(x, norm_ffn1, w_ffn1_gate, w_ffn1_up, w_ffn1_down, norm_mix, w_in, b_forget, w_gate, b_gate,
           w_up_a, w_up_b, w_out, norm_ffn2, w_ffn2_gate, w_ffn2_up, w_ffn2_down, norm_final):
    batch, seq, _ = x.shape
    depth = norm_ffn1.shape[0]
    x2d = x.reshape(batch * seq, D_MODEL)
    gf = norm_final.reshape(1, D_MODEL)

    row = lax.broadcasted_iota(jnp.int32, (TM_PROJ, TM_PROJ), 0)
    col = lax.broadcasted_iota(jnp.int32, (TM_PROJ, TM_PROJ), 1)
    tri_l = (col <= row).astype(BF16)
    rk = lax.broadcasted_iota(jnp.int32, (TK, TK), 0)
    ck = lax.broadcasted_iota(jnp.int32, (TK, TK), 1)
    tri_u = (rk >= ck).astype(BF16)

    for l in range(depth):
        last = l == depth - 1
        x2d = _ffn(x2d, norm_ffn1[l].reshape(1, D_MODEL), w_ffn1_gate[l], w_ffn1_up[l], w_ffn1_down[l], gf,
                   final_norm=False)

        n_f = w_in.shape[2] - 6 * WIDTH
        w_f = jnp.pad(w_in[l][:, 6 * WIDTH:], ((0, 0), (0, LANES - n_f))).astype(BF16)
        bf_p = jnp.pad(b_forget[l], (0, LANES - n_f)).reshape(1, LANES)
        g_mix = norm_mix[l].reshape(1, D_MODEL)
        qkv, cf_col, cf_row = _proj(x2d, g_mix, w_in[l], w_f, bf_p, tri_l, batch, seq)
        y_a, y_b = _attention(qkv, tri_u, cf_col, cf_row, batch, seq)
        x2d = _post(x2d, y_a, y_b, g_mix, w_gate[l], b_gate[l].reshape(1, 2 * D_MODEL), w_up_a[l], w_up_b[l], w_out[l])

        x2d = _ffn(x2d, norm_ffn2[l].reshape(1, D_MODEL), w_ffn2_gate[l], w_ffn2_up[l], w_ffn2_down[l], gf,
                   final_norm=last)
    return x2d.reshape(batch, seq, D_MODEL)
```

```python
import functools

import jax
import jax.numpy as jnp
from jax import lax
from jax.experimental import pallas as pl
from jax.experimental.pallas import tpu as pltpu

D_MODEL = 1024
D_FF = 2816
HEAD_DIM = 64
N_HEADS = 8
WIDTH = N_HEADS * HEAD_DIM
RMS_EPS = 1e-6
ATTN_SCALE = HEAD_DIM ** -0.5
LOG2E = 1.4426950408889634

LANES = 128
HEADS_PER_BLOCK = LANES // HEAD_DIM
N_HEAD_BLOCKS = WIDTH // LANES

TM_FFN = 1024
TF = 256
TM_PROJ = 512
TK = 256
TQ = 2 * TK
VMEM_LIMIT = 60 * 1024 * 1024

F32 = jnp.float32
BF16 = jnp.bfloat16
NEG_BIG = -1e30
SB_MASS_CUTOFF = 160.0


def _rms(x, g):
    ms = jnp.mean(x * x, axis=-1, keepdims=True)
    return x * lax.rsqrt(ms + RMS_EPS) * g


def _const_spec(shape):
    nd = len(shape)
    return pl.BlockSpec(shape, lambda *_: (0,) * nd, pipeline_mode=pl.Buffered(1))


def _ffn_body(x_ref, g_ref, wg_ref, wu_ref, wd_ref, gf_ref, o_ref, *, final_norm):
    x = x_ref[...]
    xn = _rms(x, g_ref[...]).astype(BF16)
    acc = jnp.zeros(x.shape, F32)
    for c in range(D_FF // TF):
        sl = slice(c * TF, (c + 1) * TF)
        g = jnp.dot(xn, wg_ref[:, sl].astype(BF16), preferred_element_type=F32)
        u = jnp.dot(xn, wu_ref[:, sl].astype(BF16), preferred_element_type=F32)
        h = (g * jax.nn.sigmoid(g) * u).astype(BF16)
        acc = acc + jnp.dot(h, wd_ref[sl, :].astype(BF16), preferred_element_type=F32)
    y = x + 0.5 * acc
    if final_norm:
        y = _rms(y, gf_ref[...])
    o_ref[...] = y


def _ffn(x2d, g, wg, wu, wd, gf, *, final_norm):
    t = x2d.shape[0]
    tile = pl.BlockSpec((TM_FFN, D_MODEL), lambda i: (i, 0))
    return pl.pallas_call(
        functools.partial(_ffn_body, final_norm=final_norm),
        grid=(t // TM_FFN,),
        in_specs=[tile, _const_spec((1, D_MODEL)), _const_spec((D_MODEL, D_FF)),
                  _const_spec((D_MODEL, D_FF)), _const_spec((D_FF, D_MODEL)),
                  _const_spec((1, D_MODEL))],
        out_specs=tile,
        out_shape=jax.ShapeDtypeStruct((t, D_MODEL), F32),
        compiler_params=pltpu.CompilerParams(dimension_semantics=("parallel",),
                                             vmem_limit_bytes=VMEM_LIMIT),
        name="ffn_final" if final_norm else "ffn",
    )(x2d, g, wg, wu, wd, gf)


def _proj_body(x_ref, g_ref, w_ref, wf_ref, bf_ref, tri_ref, qkv_ref, cfc_ref, cfr_ref, carry_ref):
    @pl.when(pl.program_id(1) == 0)
    def _():
        carry_ref[...] = jnp.zeros_like(carry_ref)

    h = _rms(x_ref[...], g_ref[...]).astype(BF16)

    def qkv_chunk(c):
        sl = slice(c * WIDTH, (c + 1) * WIDTH)
        p = jnp.dot(h, w_ref[:, sl].astype(BF16), preferred_element_type=F32)
        if c in (0, 3):
            p = p * (ATTN_SCALE * LOG2E)
        qkv_ref[:, sl] = p.astype(BF16)

    fl = jnp.dot(h, wf_ref[...], preferred_element_type=F32) + bf_ref[...]
    lf = (jnp.minimum(fl, 0.0) - jnp.log1p(jnp.exp(-jnp.abs(fl)))) * LOG2E
    hi = lf.astype(BF16)
    r1 = lf - hi.astype(F32)
    mid = r1.astype(BF16)
    lo = (r1 - mid.astype(F32)).astype(BF16)
    for c in range(3):
        qkv_chunk(c)
    cs = jnp.dot(tri_ref[...], jnp.concatenate([hi, mid, lo], axis=1), preferred_element_type=F32)
    for c in range(3, 6):
        qkv_chunk(c)
    cf = cs[:, :LANES] + cs[:, LANES:2 * LANES] + cs[:, 2 * LANES:] + carry_ref[...]
    carry_ref[...] = cf[TM_PROJ - 1:TM_PROJ, :]
    cfc_ref[...] = cf
    cfr_ref[0] = cf.T[:N_HEADS, :]


def _proj(x2d, g, w, wf, bf, tri, batch, seq):
    t = x2d.shape[0]
    ns = seq // TM_PROJ
    return pl.pallas_call(
        _proj_body,
        grid=(batch, ns),
        in_specs=[pl.BlockSpec((TM_PROJ, D_MODEL), lambda b, s: (b * ns + s, 0)),
                  _const_spec((1, D_MODEL)), _const_spec(w.shape), _const_spec((D_MODEL, LANES)),
                  _const_spec((1, LANES)), _const_spec((TM_PROJ, TM_PROJ))],
        out_specs=[pl.BlockSpec((TM_PROJ, 6 * WIDTH), lambda b, s: (b * ns + s, 0)),
                   pl.BlockSpec((TM_PROJ, LANES), lambda b, s: (b * ns + s, 0)),
                   pl.BlockSpec((1, N_HEADS, TM_PROJ), lambda b, s: (b, 0, s))],
        out_shape=[jax.ShapeDtypeStruct((t, 6 * WIDTH), BF16),
                   jax.ShapeDtypeStruct((t, LANES), F32),
                   jax.ShapeDtypeStruct((batch, N_HEADS, seq), F32)],
        scratch_shapes=[pltpu.VMEM((1, LANES), F32)],
        compiler_params=pltpu.CompilerParams(dimension_semantics=("arbitrary", "arbitrary"),
                                             vmem_limit_bytes=VMEM_LIMIT),
        name="proj",
    )(x2d, g, w, wf, bf, tri)


def _head_split(q):
    lane = lax.broadcasted_iota(jnp.int32, q.shape, 1)
    zero = jnp.zeros_like(q)
    return [jnp.where(lane < HEAD_DIM, q, zero), jnp.where(lane >= HEAD_DIM, q, zero)]


def _qk(qh, kj):
    return lax.dot_general(qh, kj, (((1,), (1,)), ((), ())), preferred_element_type=F32)


def _lane_chunks(x):
    return [x[:, c * LANES:(c + 1) * LANES] for c in range(x.shape[1] // LANES)]


def _attn_qblock(i, qa_ref, ka_ref, va_ref, qb_ref, kb_ref, vb_ref, u_ref, cfc_ref, cfr_ref, ya_ref, yb_ref,
                 acc_a, s_ref, m_ref, l_ref, acc_b):
    hp = pl.program_id(1)
    heads = range(HEADS_PER_BLOCK)
    lo_rows, hi_rows, all_rows = slice(0, TK), slice(TK, TQ), slice(0, TQ)
    t_idx = lax.broadcasted_iota(jnp.int32, (TK, TK), 0)
    s_idx = lax.broadcasted_iota(jnp.int32, (TK, TK), 1)
    strict = s_idx < t_idx
    causal = s_idx <= t_idx
    lane = lax.broadcasted_iota(jnp.int32, (TQ, LANES), 1)
    diag_off = pl.multiple_of(i * TQ, TQ)

    qa = _head_split(qa_ref[pl.ds(diag_off, TQ), :])

    def key_off(j):
        return pl.multiple_of(j * TK, TK)

    def sb_qk(tasks):
        return [_qk(qa[h][rows], ka_ref[pl.ds(key_off(j), TK), :]) for j, h, rows, _ in tasks]

    def sb_cumsum(tasks, zs):
        cs = []
        for z, (_, _, _, diag) in zip(zs, tasks):
            sp = jnp.maximum(z, 0.0) + jnp.log2(1.0 + jnp.exp2(jnp.minimum(z, -z)))
            if diag:
                sp = jnp.where(strict, sp, 0.0)
            cs.append(jnp.dot(sp.astype(BF16), u_ref[...], preferred_element_type=F32))
        return cs

    def sb_pv(tasks, zs, cs):
        outs = []
        for z, c, (j, _, _, diag) in zip(zs, cs, tasks):
            w = jnp.exp2(z - c)
            if diag:
                w = jnp.where(strict, w, 0.0)
            o = jnp.dot(w.astype(BF16), va_ref[pl.ds(key_off(j), TK), :], preferred_element_type=F32)
            outs.append((o, c[:, 0:1]))
        return outs

    def sb_block_tasks(j):
        return [(j, h, rows, False) for h in heads for rows in (lo_rows, hi_rows)]

    def sb_add_block(res, mass, gate=None):
        new_mass = []
        for h in heads:
            (o_l, t_l), (o_h, t_h) = res[2 * h:2 * h + 2]
            o, t = jnp.concatenate([o_l, o_h], axis=0), jnp.concatenate([t_l, t_h], axis=0)
            scale = jnp.exp2(-mass[h])
            if gate is not None:
                scale, t = scale * gate, t * gate
            acc_a[h] += o * scale
            new_mass.append(mass[h] + t)
        return new_mass

    qb = _head_split(qb_ref[pl.ds(diag_off, TQ), :])
    cfc = cfc_ref[pl.ds(diag_off, TQ), :]
    fox_heads = [hp * HEADS_PER_BLOCK + h for h in heads]
    cf_t = [jnp.broadcast_to(jnp.sum(jnp.where(lane == hd, cfc, 0.0), axis=1, keepdims=True), (TQ, LANES))
            for hd in fox_heads]

    def fox_qk(off, width, h, rows):
        return _qk(qb[h][rows], kb_ref[pl.ds(off, width), :])

    def fox_logits(z, off, width, h, rows):
        cf_s = cfr_ref[0, pl.ds(fox_heads[h], 1), pl.ds(off, width)]
        return jnp.concatenate([zc + cf_t[h][rows] for zc in _lane_chunks(z)], axis=1) - cf_s

    def chunk_max(s):
        return functools.reduce(jnp.maximum, _lane_chunks(s))

    def fox_probs(off, width, h, rows):
        s = s_ref[h, rows, pl.ds(off, width)]
        m = m_ref[h, rows, :]
        return jnp.concatenate([jnp.exp2(sc - m) for sc in _lane_chunks(s)], axis=1).astype(BF16)

    def fox_pv(p, off, width):
        v1 = jnp.concatenate([vb_ref[pl.ds(off, width), :], jnp.ones((width, LANES), BF16)], axis=1)
        o = jnp.dot(p, v1, preferred_element_type=F32)
        return o[:, :LANES], o[:, LANES:]

    diag_tasks = [t for h in heads for t in ((2 * i + 1, h, hi_rows, True), (2 * i, h, hi_rows, False),
                                             (2 * i, h, lo_rows, True))]
    zs = sb_qk(diag_tasks)
    fz = [(fox_qk(diag_off, TQ, h, hi_rows), fox_qk(diag_off, TK, h, lo_rows)) for h in heads]
    cs = sb_cumsum(diag_tasks, zs)
    for h in heads:
        s_hi = fox_logits(fz[h][0], diag_off, TQ, h, hi_rows)
        s_hi = jnp.concatenate([s_hi[:, :TK], jnp.where(causal, s_hi[:, TK:], NEG_BIG)], axis=1)
        s_lo = jnp.where(causal, fox_logits(fz[h][1], diag_off, TK, h, lo_rows), NEG_BIG)
        s_ref[h, hi_rows, pl.ds(diag_off, TQ)] = s_hi
        s_ref[h, lo_rows, pl.ds(diag_off, TK)] = s_lo
        m_ref[h, hi_rows, :] = chunk_max(s_hi)
        m_ref[h, lo_rows, :] = chunk_max(s_lo)
    res = sb_pv(diag_tasks, zs, cs)
    mass = []
    for h in heads:
        (o_b, r_b), (o_a, r_a), (o_l, r_l) = res[3 * h:3 * h + 3]
        acc_a[h, hi_rows, :] = o_b + o_a * jnp.exp2(-r_b)
        acc_a[h, lo_rows, :] = o_l
        mass.append(jnp.concatenate([r_l, r_a + r_b], axis=0))

    def pass1(jj, _):
        off = pl.multiple_of(jj * TQ, TQ)
        for h in heads:
            s = fox_logits(fox_qk(off, TQ, h, all_rows), off, TQ, h, all_rows)
            s_ref[h, :, pl.ds(off, TQ)] = s
            m_ref[h] = jnp.maximum(m_ref[h], chunk_max(s))
        return 0

    lax.fori_loop(0, i, pass1, 0)

    for h in heads:
        m_ref[h] = jnp.broadcast_to(jnp.max(m_ref[h], axis=1, keepdims=True), (TQ, LANES))
    first_tasks = sb_block_tasks(jnp.maximum(2 * i - 1, 0))
    zs = sb_qk(first_tasks)
    probs = [(fox_probs(diag_off, TQ, h, hi_rows), fox_probs(diag_off, TK, h, lo_rows)) for h in heads]
    for h in heads:
        acc_b[h, hi_rows, :], l_ref[h, hi_rows, :] = fox_pv(probs[h][0], diag_off, TQ)
        acc_b[h, lo_rows, :], l_ref[h, lo_rows, :] = fox_pv(probs[h][1], diag_off, TK)
    cs = sb_cumsum(first_tasks, zs)
    res = sb_pv(first_tasks, zs, cs)
    mass = sb_add_block(res, mass, gate=jnp.where(i > 0, 1.0, 0.0).astype(F32))

    def pass2(jj, _):
        off = pl.multiple_of(jj * TQ, TQ)
        for h in heads:
            o, l = fox_pv(fox_probs(off, TQ, h, all_rows), off, TQ)
            acc_b[h] += o
            l_ref[h] += l
        return 0

    lax.fori_loop(0, i, pass2, 0)

    def min_mass(mass):
        return jnp.min(functools.reduce(jnp.minimum, mass))

    def live(carry):
        return jnp.logical_and(carry[0] >= 0, carry[1] < SB_MASS_CUTOFF)

    def step(carry):
        j, mass = carry[0], carry[2:]
        tasks = sb_block_tasks(j)
        zs = sb_qk(tasks)
        new_mass = sb_add_block(sb_pv(tasks, zs, sb_cumsum(tasks, zs)), mass)
        return (j - 1, min_mass(new_mass), *new_mass)

    lax.while_loop(live, step, (2 * i - 2, min_mass(mass), *mass))

    ya_ref[pl.ds(diag_off, TQ), :] = jnp.where(lane < HEAD_DIM, acc_a[0], acc_a[1]).astype(BF16)
    yb = [acc_b[h] / l_ref[h] for h in heads]
    yb_ref[pl.ds(diag_off, TQ), :] = jnp.where(lane < HEAD_DIM, yb[0], yb[1]).astype(BF16)


def _attn_body(*refs, nq):
    def qblock(i, carry):
        _attn_qblock(i, *refs)
        return carry

    lax.fori_loop(0, nq, qblock, 0)


def _attention(qkv, tri_u, cf_col, cf_row, batch, seq):
    nb = N_HEAD_BLOCKS

    def seq_spec(col):
        return pl.BlockSpec((seq, LANES), lambda b, hp: (b, col + hp))

    y_spec = seq_spec(0)
    y_shape = jax.ShapeDtypeStruct((batch * seq, WIDTH), BF16)
    pair_tile = pltpu.VMEM((HEADS_PER_BLOCK, TQ, LANES), F32)
    return pl.pallas_call(
        functools.partial(_attn_body, nq=seq // TQ),
        grid=(batch, nb),
        in_specs=[seq_spec(c * nb) for c in range(6)]
        + [_const_spec((TK, TK)),
           pl.BlockSpec((seq, LANES), lambda b, hp: (b, 0)),
           pl.BlockSpec((1, N_HEADS, seq), lambda b, hp: (b, 0, 0))],
        out_specs=[y_spec, y_spec],
        out_shape=[y_shape, y_shape],
        scratch_shapes=[pair_tile, pltpu.VMEM((HEADS_PER_BLOCK, TQ, seq), F32), pair_tile, pair_tile, pair_tile],
        compiler_params=pltpu.CompilerParams(dimension_semantics=("parallel", "parallel"),
                                             vmem_limit_bytes=VMEM_LIMIT),
        name="attention",
    )(qkv, qkv, qkv, qkv, qkv, qkv, tri_u, cf_col, cf_row)


def _post_body(x_ref, ya_ref, yb_ref, g_ref, wgate_ref, bgate_ref, wua_ref, wub_ref, wout_ref, o_ref):
    x = x_ref[...]
    h = _rms(x, g_ref[...]).astype(BF16)
    ya = ya_ref[...]
    yb = yb_ref[...]
    acc = jnp.zeros(x.shape, F32)
    nchunk = D_MODEL // WIDTH
    for c in range(nchunk):
        sl_a = slice(c * WIDTH, (c + 1) * WIDTH)
        sl_b = slice(D_MODEL + c * WIDTH, D_MODEL + (c + 1) * WIDTH)
        ga = jax.nn.sigmoid(jnp.dot(h, wgate_ref[:, sl_a].astype(BF16), preferred_element_type=F32)
                            + bgate_ref[:, sl_a])
        gb = jax.nn.sigmoid(jnp.dot(h, wgate_ref[:, sl_b].astype(BF16), preferred_element_type=F32)
                            + bgate_ref[:, sl_b])
        ua = jnp.dot(ya, wua_ref[:, sl_a].astype(BF16), preferred_element_type=F32)
        ub = jnp.dot(yb, wub_ref[:, sl_a].astype(BF16), preferred_element_type=F32)
        mixed = (ga * ua + gb * ub).astype(BF16)
        acc = acc + jnp.dot(mixed, wout_ref[sl_a, :].astype(BF16), preferred_element_type=F32)
    o_ref[...] = x + acc


def _post(x2d, ya, yb, g, wgate, bgate, wua, wub, wout):
    t = x2d.shape[0]
    tile = pl.BlockSpec((TM_FFN, D_MODEL), lambda i: (i, 0))
    ytile = pl.BlockSpec((TM_FFN, WIDTH), lambda i: (i, 0))
    return pl.pallas_call(
        _post_body,
        grid=(t // TM_FFN,),
        in_specs=[tile, ytile, ytile, _const_spec((1, D_MODEL)), _const_spec((D_MODEL, 2 * D_MODEL)),
                  _const_spec((1, 2 * D_MODEL)), _const_spec((WIDTH, D_MODEL)),
                  _const_spec((WIDTH, D_MODEL)), _const_spec((D_MODEL, D_MODEL))],
        out_specs=tile,
        out_shape=jax.ShapeDtypeStruct((t, D_MODEL), F32),
        compiler_params=pltpu.CompilerParams(dimension_semantics=("parallel",),
                                             vmem_limit_bytes=VMEM_LIMIT),
        name="post",
    )(x2d, ya, yb, g, wgate, bgate, wua, wub, wout)


def kernel(x, norm_ffn1, w_ffn1_gate, w_ffn1_up, w_ffn1_down, norm_mix, w_in, b_forget, w_gate, b_gate,
           w_up_a, w_up_b, w_out, norm_ffn2, w_ffn2_gate, w_ffn2_up, w_ffn2_down, norm_final):
    batch, seq, _ = x.shape
    depth = norm_ffn1.shape[0]
    x2d = x.reshape(batch * seq, D_MODEL)
    gf = norm_final.reshape(1, D_MODEL)

    row = lax.broadcasted_iota(jnp.int32, (TM_PROJ, TM_PROJ), 0)
    col = lax.broadcasted_iota(jnp.int32, (TM_PROJ, TM_PROJ), 1)
    tri_l = (col <= row).astype(BF16)
    rk = lax.broadcasted_iota(jnp.int32, (TK, TK), 0)
    ck = lax.broadcasted_iota(jnp.int32, (TK, TK), 1)
    tri_u = (rk >= ck).astype(BF16)

    for l in range(depth):
        last = l == depth - 1
        x2d = _ffn(x2d, norm_ffn1[l].reshape(1, D_MODEL), w_ffn1_gate[l], w_ffn1_up[l], w_ffn1_down[l], gf,
                   final_norm=False)

        n_f = w_in.shape[2] - 6 * WIDTH
        w_f = jnp.pad(w_in[l][:, 6 * WIDTH:], ((0, 0), (0, LANES - n_f))).astype(BF16)
        bf_p = jnp.pad(b_forget[l], (0, LANES - n_f)).reshape(1, LANES)
        g_mix = norm_mix[l].reshape(1, D_MODEL)
        qkv, cf_col, cf_row = _proj(x2d, g_mix, w_in[l], w_f, bf_p, tri_l, batch, seq)
        y_a, y_b = _attention(qkv, tri_u, cf_col, cf_row, batch, seq)
        x2d = _post(x2d, y_a, y_b, g_mix, w_gate[l], b_gate[l].reshape(1, 2 * D_MODEL), w_up_a[l], w_up_b[l], w_out[l])

        x2d = _ffn(x2d, norm_ffn2[l].reshape(1, D_MODEL), w_ffn2_gate[l], w_ffn2_up[l], w_ffn2_down[l], gf,
                   final_norm=last)
    return x2d.reshape(batch, seq, D_MODEL)
```

```python
import functools

import jax
import jax.numpy as jnp
from jax import lax
from jax.experimental import pallas as pl
from jax.experimental.pallas import tpu as pltpu

D_MODEL = 1024
D_FF = 2816
HEAD_DIM = 64
N_HEADS = 8
WIDTH = N_HEADS * HEAD_DIM
RMS_EPS = 1e-6
ATTN_SCALE = HEAD_DIM ** -0.5
LOG2E = 1.4426950408889634

LANES = 128
HEADS_PER_BLOCK = LANES // HEAD_DIM
N_HEAD_BLOCKS = WIDTH // LANES

TM_FFN = 1024
TF = 256
TM_PROJ = 512
TK = 256
TQ = 2 * TK
VMEM_LIMIT = 60 * 1024 * 1024

F32 = jnp.float32
BF16 = jnp.bfloat16
NEG_BIG = -1e30
SB_MASS_CUTOFF = 160.0


def _rms(x, g):
    ms = jnp.mean(x * x, axis=-1, keepdims=True)
    return x * lax.rsqrt(ms + RMS_EPS) * g


def _const_spec(shape):
    nd = len(shape)
    return pl.BlockSpec(shape, lambda *_: (0,) * nd, pipeline_mode=pl.Buffered(1))


def _ffn_body(x_ref, g_ref, wg_ref, wu_ref, wd_ref, gf_ref, o_ref, *, final_norm):
    x = x_ref[...]
    xn = _rms(x, g_ref[...]).astype(BF16)
    acc = jnp.zeros(x.shape, F32)
    for c in range(D_FF // TF):
        sl = slice(c * TF, (c + 1) * TF)
        g = jnp.dot(xn, wg_ref[:, sl].astype(BF16), preferred_element_type=F32)
        u = jnp.dot(xn, wu_ref[:, sl].astype(BF16), preferred_element_type=F32)
        h = (g * jax.nn.sigmoid(g) * u).astype(BF16)
        acc = acc + jnp.dot(h, wd_ref[sl, :].astype(BF16), preferred_element_type=F32)
    y = x + 0.5 * acc
    if final_norm:
        y = _rms(y, gf_ref[...])
    o_ref[...] = y


def _ffn(x2d, g, wg, wu, wd, gf, *, final_norm):
    t = x2d.shape[0]
    tile = pl.BlockSpec((TM_FFN, D_MODEL), lambda i: (i, 0))
    return pl.pallas_call(
        functools.partial(_ffn_body, final_norm=final_norm),
        grid=(t // TM_FFN,),
        in_specs=[tile, _const_spec((1, D_MODEL)), _const_spec((D_MODEL, D_FF)),
                  _const_spec((D_MODEL, D_FF)), _const_spec((D_FF, D_MODEL)),
                  _const_spec((1, D_MODEL))],
        out_specs=tile,
        out_shape=jax.ShapeDtypeStruct((t, D_MODEL), F32),
        compiler_params=pltpu.CompilerParams(dimension_semantics=("parallel",),
                                             vmem_limit_bytes=VMEM_LIMIT),
        name="ffn_final" if final_norm else "ffn",
    )(x2d, g, wg, wu, wd, gf)


def _proj_body(x_ref, g_ref, w_ref, wf_ref, bf_ref, tri_ref, qkv_ref, cfc_ref, cfr_ref, carry_ref):
    @pl.when(pl.program_id(1) == 0)
    def _():
        carry_ref[...] = jnp.zeros_like(carry_ref)

    h = _rms(x_ref[...], g_ref[...]).astype(BF16)

    def qkv_chunk(c):
        sl = slice(c * WIDTH, (c + 1) * WIDTH)
        p = jnp.dot(h, w_ref[:, sl].astype(BF16), preferred_element_type=F32)
        if c in (0, 3):
            p = p * (ATTN_SCALE * LOG2E)
        qkv_ref[:, sl] = p.astype(BF16)

    fl = jnp.dot(h, wf_ref[...], preferred_element_type=F32) + bf_ref[...]
    lf = (jnp.minimum(fl, 0.0) - jnp.log1p(jnp.exp(-jnp.abs(fl)))) * LOG2E
    hi = lf.astype(BF16)
    r1 = lf - hi.astype(F32)
    mid = r1.astype(BF16)
    lo = (r1 - mid.astype(F32)).astype(BF16)
    for c in range(3):
        qkv_chunk(c)
    cs = jnp.dot(tri_ref[...], jnp.concatenate([hi, mid, lo], axis=1), preferred_element_type=F32)
    for c in range(3, 6):
        qkv_chunk(c)
    cf = cs[:, :LANES] + cs[:, LANES:2 * LANES] + cs[:, 2 * LANES:] + carry_ref[...]
    carry_ref[...] = cf[TM_PROJ - 1:TM_PROJ, :]
    cfc_ref[...] = cf
    cfr_ref[0] = cf.T[:N_HEADS, :]


def _proj(x2d, g, w, wf, bf, tri, batch, seq):
    t = x2d.shape[0]
    ns = seq // TM_PROJ
    return pl.pallas_call(
        _proj_body,
        grid=(batch, ns),
        in_specs=[pl.BlockSpec((TM_PROJ, D_MODEL), lambda b, s: (b * ns + s, 0)),
                  _const_spec((1, D_MODEL)), _const_spec(w.shape), _const_spec((D_MODEL, LANES)),
                  _const_spec((1, LANES)), _const_spec((TM_PROJ, TM_PROJ))],
        out_specs=[pl.BlockSpec((TM_PROJ, 6 * WIDTH), lambda b, s: (b * ns + s, 0)),
                   pl.BlockSpec((TM_PROJ, LANES), lambda b, s: (b * ns + s, 0)),
                   pl.BlockSpec((1, N_HEADS, TM_PROJ), lambda b, s: (b, 0, s))],
        out_shape=[jax.ShapeDtypeStruct((t, 6 * WIDTH), BF16),
                   jax.ShapeDtypeStruct((t, LANES), F32),
                   jax.ShapeDtypeStruct((batch, N_HEADS, seq), F32)],
        scratch_shapes=[pltpu.VMEM((1, LANES), F32)],
        compiler_params=pltpu.CompilerParams(dimension_semantics=("arbitrary", "arbitrary"),
                                             vmem_limit_bytes=VMEM_LIMIT),
        name="proj",
    )(x2d, g, w, wf, bf, tri)


def _head_split(q):
    lane = lax.broadcasted_iota(jnp.int32, q.shape, 1)
    zero = jnp.zeros_like(q)
    return [jnp.where(lane < HEAD_DIM, q, zero), jnp.where(lane >= HEAD_DIM, q, zero)]


def _qk(qh, kj):
    return lax.dot_general(qh, kj, (((1,), (1,)), ((), ())), preferred_element_type=F32)


def _lane_chunks(x):
    return [x[:, c * LANES:(c + 1) * LANES] for c in range(x.shape[1] // LANES)]


def _attn_qblock(i, qa_ref, ka_ref, va_ref, qb_ref, kb_ref, vb_ref, u_ref, cfc_ref, cfr_ref, ya_ref, yb_ref,
                 acc_a, s_ref, m_ref, l_ref, acc_b):
    hp = pl.program_id(1)
    heads = range(HEADS_PER_BLOCK)
    lo_rows, hi_rows, all_rows = slice(0, TK), slice(TK, TQ), slice(0, TQ)
    t_idx = lax.broadcasted_iota(jnp.int32, (TK, TK), 0)
    s_idx = lax.broadcasted_iota(jnp.int32, (TK, TK), 1)
    strict = s_idx < t_idx
    causal = s_idx <= t_idx
    lane = lax.broadcasted_iota(jnp.int32, (TQ, LANES), 1)
    diag_off = pl.multiple_of(i * TQ, TQ)

    qa = _head_split(qa_ref[pl.ds(diag_off, TQ), :])

    def key_off(j):
        return pl.multiple_of(j * TK, TK)

    def sb_qk(tasks):
        return [_qk(qa[h][rows], ka_ref[pl.ds(key_off(j), TK), :]) for j, h, rows, _ in tasks]

    def sb_cumsum(tasks, zs):
        cs = []
        for z, (_, _, _, diag) in zip(zs, tasks):
            sp = jnp.maximum(z, 0.0) + jnp.log2(1.0 + jnp.exp2(jnp.minimum(z, -z)))
            if diag:
                sp = jnp.where(strict, sp, 0.0)
            cs.append(jnp.dot(sp.astype(BF16), u_ref[...], preferred_element_type=F32))
        return cs

    def sb_pv(tasks, zs, cs):
        outs = []
        for z, c, (j, _, _, diag) in zip(zs, cs, tasks):
            w = jnp.exp2(z - c)
            if diag:
                w = jnp.where(strict, w, 0.0)
            o = jnp.dot(w.astype(BF16), va_ref[pl.ds(key_off(j), TK), :], preferred_element_type=F32)
            outs.append((o, c[:, 0:1]))
        return outs

    def sb_half_tasks(j, rows):
        return [(j, h, rows, False) for h in heads]

    def sb_add(res, rows, mass, gate=None):
        new_mass = []
        for h in heads:
            o, t = res[h]
            scale = jnp.exp2(-mass[h])
            if gate is not None:
                scale, t = scale * gate, t * gate
            acc_a[h, rows, :] += o * scale
            new_mass.append(mass[h] + t)
        return new_mass

    def sb_stream(rows, j_first, mass):
        def min_mass(mass):
            return jnp.min(functools.reduce(jnp.minimum, mass))

        def live(carry):
            return jnp.logical_and(carry[0] >= 0, carry[1] < SB_MASS_CUTOFF)

        def step(carry):
            j, mass = carry[0], carry[2:]
            tasks = sb_half_tasks(j, rows)
            zs = sb_qk(tasks)
            new_mass = sb_add(sb_pv(tasks, zs, sb_cumsum(tasks, zs)), rows, mass)
            return (j - 1, min_mass(new_mass), *new_mass)

        lax.while_loop(live, step, (j_first, min_mass(mass), *mass))

    qb = _head_split(qb_ref[pl.ds(diag_off, TQ), :])
    cfc = cfc_ref[pl.ds(diag_off, TQ), :]
    fox_heads = [hp * HEADS_PER_BLOCK + h for h in heads]
    cf_t = [jnp.broadcast_to(jnp.sum(jnp.where(lane == hd, cfc, 0.0), axis=1, keepdims=True), (TQ, LANES))
            for hd in fox_heads]

    def fox_qk(off, width, h, rows):
        return _qk(qb[h][rows], kb_ref[pl.ds(off, width), :])

    def fox_logits(z, off, width, h, rows):
        cf_s = cfr_ref[0, pl.ds(fox_heads[h], 1), pl.ds(off, width)]
        return jnp.concatenate([zc + cf_t[h][rows] for zc in _lane_chunks(z)], axis=1) - cf_s

    def chunk_max(s):
        return functools.reduce(jnp.maximum, _lane_chunks(s))

    def fox_probs(off, width, h, rows):
        s = s_ref[h, rows, pl.ds(off, width)]
        m = m_ref[h, rows, :]
        return jnp.concatenate([jnp.exp2(sc - m) for sc in _lane_chunks(s)], axis=1).astype(BF16)

    def fox_pv(p, off, width):
        v1 = jnp.concatenate([vb_ref[pl.ds(off, width), :], jnp.ones((width, LANES), BF16)], axis=1)
        o = jnp.dot(p, v1, preferred_element_type=F32)
        return o[:, :LANES], o[:, LANES:]

    diag_tasks = [t for h in heads for t in ((2 * i + 1, h, hi_rows, True), (2 * i, h, hi_rows, False),
                                             (2 * i, h, lo_rows, True))]
    zs = sb_qk(diag_tasks)
    fz = [(fox_qk(diag_off, TQ, h, hi_rows), fox_qk(diag_off, TK, h, lo_rows)) for h in heads]
    cs = sb_cumsum(diag_tasks, zs)
    for h in heads:
        s_hi = fox_logits(fz[h][0], diag_off, TQ, h, hi_rows)
        s_hi = jnp.concatenate([s_hi[:, :TK], jnp.where(causal, s_hi[:, TK:], NEG_BIG)], axis=1)
        s_lo = jnp.where(causal, fox_logits(fz[h][1], diag_off, TK, h, lo_rows), NEG_BIG)
        s_ref[h, hi_rows, pl.ds(diag_off, TQ)] = s_hi
        s_ref[h, lo_rows, pl.ds(diag_off, TK)] = s_lo
        m_ref[h, hi_rows, :] = chunk_max(s_hi)
        m_ref[h, lo_rows, :] = chunk_max(s_lo)
    res = sb_pv(diag_tasks, zs, cs)
    mass_lo, mass_hi = [], []
    for h in heads:
        (o_b, r_b), (o_a, r_a), (o_l, r_l) = res[3 * h:3 * h + 3]
        acc_a[h, hi_rows, :] = o_b + o_a * jnp.exp2(-r_b)
        acc_a[h, lo_rows, :] = o_l
        mass_lo.append(r_l)
        mass_hi.append(r_a + r_b)

    def pass1(jj, _):
        off = pl.multiple_of(jj * TQ, TQ)
        for h in heads:
            s = fox_logits(fox_qk(off, TQ, h, all_rows), off, TQ, h, all_rows)
            s_ref[h, :, pl.ds(off, TQ)] = s
            m_ref[h] = jnp.maximum(m_ref[h], chunk_max(s))
        return 0

    lax.fori_loop(0, i, pass1, 0)

    for h in heads:
        m_ref[h] = jnp.broadcast_to(jnp.max(m_ref[h], axis=1, keepdims=True), (TQ, LANES))
    first_tasks = sb_half_tasks(jnp.maximum(2 * i - 1, 0), lo_rows)
    zs = sb_qk(first_tasks)
    probs = [(fox_probs(diag_off, TQ, h, hi_rows), fox_probs(diag_off, TK, h, lo_rows)) for h in heads]
    for h in heads:
        acc_b[h, hi_rows, :], l_ref[h, hi_rows, :] = fox_pv(probs[h][0], diag_off, TQ)
        acc_b[h, lo_rows, :], l_ref[h, lo_rows, :] = fox_pv(probs[h][1], diag_off, TK)
    cs = sb_cumsum(first_tasks, zs)
    res = sb_pv(first_tasks, zs, cs)
    mass_lo = sb_add(res, lo_rows, mass_lo, gate=jnp.where(i > 0, 1.0, 0.0).astype(F32))

    def pass2(jj, _):
        off = pl.multiple_of(jj * TQ, TQ)
        for h in heads:
            o, l = fox_pv(fox_probs(off, TQ, h, all_rows), off, TQ)
            acc_b[h] += o
            l_ref[h] += l
        return 0

    lax.fori_loop(0, i, pass2, 0)

    sb_stream(lo_rows, 2 * i - 2, mass_lo)
    sb_stream(hi_rows, 2 * i - 1, mass_hi)

    ya_ref[pl.ds(diag_off, TQ), :] = jnp.where(lane < HEAD_DIM, acc_a[0], acc_a[1]).astype(BF16)
    yb = [acc_b[h] / l_ref[h] for h in heads]
    yb_ref[pl.ds(diag_off, TQ), :] = jnp.where(lane < HEAD_DIM, yb[0], yb[1]).astype(BF16)


def _attn_body(*refs, nq):
    def qblock(i, carry):
        _attn_qblock(i, *refs)
        return carry

    lax.fori_loop(0, nq, qblock, 0)


def _attention(qkv, tri_u, cf_col, cf_row, batch, seq):
    nb = N_HEAD_BLOCKS

    def seq_spec(col):
        return pl.BlockSpec((seq, LANES), lambda b, hp: (b, col + hp))

    y_spec = seq_spec(0)
    y_shape = jax.ShapeDtypeStruct((batch * seq, WIDTH), BF16)
    pair_tile = pltpu.VMEM((HEADS_PER_BLOCK, TQ, LANES), F32)
    return pl.pallas_call(
        functools.partial(_attn_body, nq=seq // TQ),
        grid=(batch, nb),
        in_specs=[seq_spec(c * nb) for c in range(6)]
        + [_const_spec((TK, TK)),
           pl.BlockSpec((seq, LANES), lambda b, hp: (b, 0)),
           pl.BlockSpec((1, N_HEADS, seq), lambda b, hp: (b, 0, 0))],
        out_specs=[y_spec, y_spec],
        out_shape=[y_shape, y_shape],
        scratch_shapes=[pair_tile, pltpu.VMEM((HEADS_PER_BLOCK, TQ, seq), F32), pair_tile, pair_tile, pair_tile],
        compiler_params=pltpu.CompilerParams(dimension_semantics=("parallel", "parallel"),
                                             vmem_limit_bytes=VMEM_LIMIT),
        name="attention",
    )(qkv, qkv, qkv, qkv, qkv, qkv, tri_u, cf_col, cf_row)


def _post_body(x_ref, ya_ref, yb_ref, g_ref, wgate_ref, bgate_ref, wua_ref, wub_ref, wout_ref, o_ref):
    x = x_ref[...]
    h = _rms(x, g_ref[...]).astype(BF16)
    ya = ya_ref[...]
    yb = yb_ref[...]
    acc = jnp.zeros(x.shape, F32)
    nchunk = D_MODEL // WIDTH
    for c in range(nchunk):
        sl_a = slice(c * WIDTH, (c + 1) * WIDTH)
        sl_b = slice(D_MODEL + c * WIDTH, D_MODEL + (c + 1) * WIDTH)
        ga = jax.nn.sigmoid(jnp.dot(h, wgate_ref[:, sl_a].astype(BF16), preferred_element_type=F32)
                            + bgate_ref[:, sl_a])
        gb = jax.nn.sigmoid(jnp.dot(h, wgate_ref[:, sl_b].astype(BF16), preferred_element_type=F32)
                            + bgate_ref[:, sl_b])
        ua = jnp.dot(ya, wua_ref[:, sl_a].astype(BF16), preferred_element_type=F32)
        ub = jnp.dot(yb, wub_ref[:, sl_a].astype(BF16), preferred_element_type=F32)
        mixed = (ga * ua + gb * ub).astype(BF16)
        acc = acc + jnp.dot(mixed, wout_ref[sl_a, :].astype(BF16), preferred_element_type=F32)
    o_ref[...] = x + acc


def _post(x2d, ya, yb, g, wgate, bgate, wua, wub, wout):
    t = x2d.shape[0]
    tile = pl.BlockSpec((TM_FFN, D_MODEL), lambda i: (i, 0))
    ytile = pl.BlockSpec((TM_FFN, WIDTH), lambda i: (i, 0))
    return pl.pallas_call(
        _post_body,
        grid=(t // TM_FFN,),
        in_specs=[tile, ytile, ytile, _const_spec((1, D_MODEL)), _const_spec((D_MODEL, 2 * D_MODEL)),
                  _const_spec((1, 2 * D_MODEL)), _const_spec((WIDTH, D_MODEL)),
                  _const_spec((WIDTH, D_MODEL)), _const_spec((D_MODEL, D_MODEL))],
        out_specs=tile,
        out_shape=jax.ShapeDtypeStruct((t, D_MODEL), F32),
        compiler_params=pltpu.CompilerParams(dimension_semantics=("parallel",),
                                             vmem_limit_bytes=VMEM_LIMIT),
        name="post",
    )(x2d, ya, yb, g, wgate, bgate, wua, wub, wout)


def kernel(x, norm_ffn1, w_ffn1_gate, w_ffn1_up, w_ffn1_down, norm_mix, w_in, b_forget, w_gate, b_gate,
           w_up_a, w_up_b, w_out, norm_ffn2, w_ffn2_gate, w_ffn2_up, w_ffn2_down, norm_final):
    batch, seq, _ = x.shape
    depth = norm_ffn1.shape[0]
    x2d = x.reshape(batch * seq, D_MODEL)
    gf = norm_final.reshape(1, D_MODEL)

    row = lax.broadcasted_iota(jnp.int32, (TM_PROJ, TM_PROJ), 0)
    col = lax.broadcasted_iota(jnp.int32, (TM_PROJ, TM_PROJ), 1)
    tri_l = (col <= row).astype(BF16)
    rk = lax.broadcasted_iota(jnp.int32, (TK, TK), 0)
    ck = lax.broadcasted_iota(jnp.int32, (TK, TK), 1)
    tri_u = (rk >= ck).astype(BF16)

    for l in range(depth):
        last = l == depth - 1
        x2d = _ffn(x2d, norm_ffn1[l].reshape(1, D_MODEL), w_ffn1_gate[l], w_ffn1_up[l], w_ffn1_down[l], gf,
                   final_norm=False)

        n_f = w_in.shape[2] - 6 * WIDTH
        w_f = jnp.pad(w_in[l][:, 6 * WIDTH:], ((0, 0), (0, LANES - n_f))).astype(BF16)
        bf_p = jnp.pad(b_forget[l], (0, LANES - n_f)).reshape(1, LANES)
        g_mix = norm_mix[l].reshape(1, D_MODEL)
        qkv, cf_col, cf_row = _proj(x2d, g_mix, w_in[l], w_f, bf_p, tri_l, batch, seq)
        y_a, y_b = _attention(qkv, tri_u, cf_col, cf_row, batch, seq)
        x2d = _post(x2d, y_a, y_b, g_mix, w_gate[l], b_gate[l].reshape(1, 2 * D_MODEL), w_up_a[l], w_up_b[l], w_out[l])

        x2d = _ffn(x2d, norm_ffn2[l].reshape(1, D_MODEL), w_ffn2_gate[l], w_ffn2_up[l], w_ffn2_down[l], gf,
                   final_norm=last)
    return x2d.reshape(batch, seq, D_MODEL)
```

```python
import functools

import jax
import jax.numpy as jnp
from jax import lax
from jax.experimental import pallas as pl
from jax.experimental.pallas import tpu as pltpu

D_MODEL = 1024
D_FF = 2816
HEAD_DIM = 64
N_HEADS = 8
WIDTH = N_HEADS * HEAD_DIM
RMS_EPS = 1e-6
ATTN_SCALE = HEAD_DIM ** -0.5
LOG2E = 1.4426950408889634

LANES = 128
HEADS_PER_BLOCK = LANES // HEAD_DIM
N_HEAD_BLOCKS = WIDTH // LANES
PAIRS_PER_STEP = 2
HEADS_PER_STEP = PAIRS_PER_STEP * HEADS_PER_BLOCK

TM_FFN = 1024
TF = 256
TM_PROJ = 512
TK = 256
TQ = 2 * TK
VMEM_LIMIT = 60 * 1024 * 1024

F32 = jnp.float32
BF16 = jnp.bfloat16
NEG_BIG = -1e30
SB_MASS_CUTOFF = 160.0


def _rms(x, g):
    ms = jnp.mean(x * x, axis=-1, keepdims=True)
    return x * lax.rsqrt(ms + RMS_EPS) * g


def _const_spec(shape):
    nd = len(shape)
    return pl.BlockSpec(shape, lambda *_: (0,) * nd, pipeline_mode=pl.Buffered(1))


def _ffn_body(x_ref, g_ref, wg_ref, wu_ref, wd_ref, gf_ref, o_ref, *, final_norm):
    x = x_ref[...]
    xn = _rms(x, g_ref[...]).astype(BF16)
    acc = jnp.zeros(x.shape, F32)
    for c in range(D_FF // TF):
        sl = slice(c * TF, (c + 1) * TF)
        g = jnp.dot(xn, wg_ref[:, sl].astype(BF16), preferred_element_type=F32)
        u = jnp.dot(xn, wu_ref[:, sl].astype(BF16), preferred_element_type=F32)
        h = (g * jax.nn.sigmoid(g) * u).astype(BF16)
        acc = acc + jnp.dot(h, wd_ref[sl, :].astype(BF16), preferred_element_type=F32)
    y = x + 0.5 * acc
    if final_norm:
        y = _rms(y, gf_ref[...])
    o_ref[...] = y


def _ffn(x2d, g, wg, wu, wd, gf, *, final_norm):
    t = x2d.shape[0]
    tile = pl.BlockSpec((TM_FFN, D_MODEL), lambda i: (i, 0))
    return pl.pallas_call(
        functools.partial(_ffn_body, final_norm=final_norm),
        grid=(t // TM_FFN,),
        in_specs=[tile, _const_spec((1, D_MODEL)), _const_spec((D_MODEL, D_FF)),
                  _const_spec((D_MODEL, D_FF)), _const_spec((D_FF, D_MODEL)),
                  _const_spec((1, D_MODEL))],
        out_specs=tile,
        out_shape=jax.ShapeDtypeStruct((t, D_MODEL), F32),
        compiler_params=pltpu.CompilerParams(dimension_semantics=("parallel",),
                                             vmem_limit_bytes=VMEM_LIMIT),
        name="ffn_final" if final_norm else "ffn",
    )(x2d, g, wg, wu, wd, gf)


def _proj_body(x_ref, g_ref, w_ref, wf_ref, bf_ref, tri_ref, qkv_ref, cfc_ref, cfr_ref, carry_ref):
    @pl.when(pl.program_id(1) == 0)
    def _():
        carry_ref[...] = jnp.zeros_like(carry_ref)

    h = _rms(x_ref[...], g_ref[...]).astype(BF16)

    def qkv_chunk(c):
        sl = slice(c * WIDTH, (c + 1) * WIDTH)
        p = jnp.dot(h, w_ref[:, sl].astype(BF16), preferred_element_type=F32)
        if c in (0, 3):
            p = p * (ATTN_SCALE * LOG2E)
        qkv_ref[:, sl] = p.astype(BF16)

    fl = jnp.dot(h, wf_ref[...], preferred_element_type=F32) + bf_ref[...]
    lf = (jnp.minimum(fl, 0.0) - jnp.log1p(jnp.exp(-jnp.abs(fl)))) * LOG2E
    hi = lf.astype(BF16)
    r1 = lf - hi.astype(F32)
    mid = r1.astype(BF16)
    lo = (r1 - mid.astype(F32)).astype(BF16)
    for c in range(3):
        qkv_chunk(c)
    cs = jnp.dot(tri_ref[...], jnp.concatenate([hi, mid, lo], axis=1), preferred_element_type=F32)
    for c in range(3, 6):
        qkv_chunk(c)
    cf = cs[:, :LANES] + cs[:, LANES:2 * LANES] + cs[:, 2 * LANES:] + carry_ref[...]
    carry_ref[...] = cf[TM_PROJ - 1:TM_PROJ, :]
    cfc_ref[...] = cf
    cfr_ref[0] = cf.T[:N_HEADS, :]


def _proj(x2d, g, w, wf, bf, tri, batch, seq):
    t = x2d.shape[0]
    ns = seq // TM_PROJ
    return pl.pallas_call(
        _proj_body,
        grid=(batch, ns),
        in_specs=[pl.BlockSpec((TM_PROJ, D_MODEL), lambda b, s: (b * ns + s, 0)),
                  _const_spec((1, D_MODEL)), _const_spec(w.shape), _const_spec((D_MODEL, LANES)),
                  _const_spec((1, LANES)), _const_spec((TM_PROJ, TM_PROJ))],
        out_specs=[pl.BlockSpec((TM_PROJ, 6 * WIDTH), lambda b, s: (b * ns + s, 0)),
                   pl.BlockSpec((TM_PROJ, LANES), lambda b, s: (b * ns + s, 0)),
                   pl.BlockSpec((1, N_HEADS, TM_PROJ), lambda b, s: (b, 0, s))],
        out_shape=[jax.ShapeDtypeStruct((t, 6 * WIDTH), BF16),
                   jax.ShapeDtypeStruct((t, LANES), F32),
                   jax.ShapeDtypeStruct((batch, N_HEADS, seq), F32)],
        scratch_shapes=[pltpu.VMEM((1, LANES), F32)],
        compiler_params=pltpu.CompilerParams(dimension_semantics=("arbitrary", "arbitrary"),
                                             vmem_limit_bytes=VMEM_LIMIT),
        name="proj",
    )(x2d, g, w, wf, bf, tri)


def _head_split(q):
    lane = lax.broadcasted_iota(jnp.int32, q.shape, 1)
    zero = jnp.zeros_like(q)
    return [jnp.where(lane < HEAD_DIM, q, zero), jnp.where(lane >= HEAD_DIM, q, zero)]


def _qk(qh, kj):
    return lax.dot_general(qh, kj, (((1,), (1,)), ((), ())), preferred_element_type=F32)


def _lane_chunks(x):
    return [x[:, c * LANES:(c + 1) * LANES] for c in range(x.shape[1] // LANES)]


def _attn_qblock(i, qa_ref, ka_ref, va_ref, qb_ref, kb_ref, vb_ref, u_ref, cfc_ref, cfr_ref, ya_ref, yb_ref,
                 acc_a, s_ref, m_ref, l_ref, acc_b):
    first_head = pl.program_id(1) * HEADS_PER_STEP
    heads = range(HEADS_PER_STEP)

    def pair_lanes(h):
        p = h // HEADS_PER_BLOCK
        return slice(p * LANES, (p + 1) * LANES)

    def head_queries(q_ref):
        return [qh for p in range(PAIRS_PER_STEP)
                for qh in _head_split(q_ref[pl.ds(diag_off, TQ), p * LANES:(p + 1) * LANES])]
    lo_rows, hi_rows, all_rows = slice(0, TK), slice(TK, TQ), slice(0, TQ)
    t_idx = lax.broadcasted_iota(jnp.int32, (TK, TK), 0)
    s_idx = lax.broadcasted_iota(jnp.int32, (TK, TK), 1)
    strict = s_idx < t_idx
    causal = s_idx <= t_idx
    lane = lax.broadcasted_iota(jnp.int32, (TQ, LANES), 1)
    diag_off = pl.multiple_of(i * TQ, TQ)

    qa = head_queries(qa_ref)

    def key_off(j):
        return pl.multiple_of(j * TK, TK)

    def sb_qk(tasks):
        return [_qk(qa[h][rows], ka_ref[pl.ds(key_off(j), TK), pair_lanes(h)]) for j, h, rows, _ in tasks]

    def sb_cumsum(tasks, zs):
        cs = []
        for z, (_, _, _, diag) in zip(zs, tasks):
            sp = jnp.maximum(z, 0.0) + jnp.log2(1.0 + jnp.exp2(jnp.minimum(z, -z)))
            if diag:
                sp = jnp.where(strict, sp, 0.0)
            cs.append(jnp.dot(sp.astype(BF16), u_ref[...], preferred_element_type=F32))
        return cs

    def sb_pv(tasks, zs, cs):
        outs = []
        for z, c, (j, h, _, diag) in zip(zs, cs, tasks):
            w = jnp.exp2(z - c)
            if diag:
                w = jnp.where(strict, w, 0.0)
            o = jnp.dot(w.astype(BF16), va_ref[pl.ds(key_off(j), TK), pair_lanes(h)], preferred_element_type=F32)
            outs.append((o, c[:, 0:1]))
        return outs

    def sb_half_tasks(j, rows):
        return [(j, h, rows, False) for h in heads]

    def sb_add(res, rows, mass, gate=None):
        new_mass = []
        for h in heads:
            o, t = res[h]
            scale = jnp.exp2(-mass[h])
            if gate is not None:
                scale, t = scale * gate, t * gate
            acc_a[h, rows, :] += o * scale
            new_mass.append(mass[h] + t)
        return new_mass

    def min_mass(mass):
        return jnp.min(functools.reduce(jnp.minimum, mass))

    def sb_stream(rows, j_first, mass, lowest):
        def live(carry):
            return jnp.logical_and(carry[0] >= 0, carry[1] < SB_MASS_CUTOFF)

        def step(carry):
            j, mass = carry[0], carry[2:]
            tasks = sb_half_tasks(j, rows)
            zs = sb_qk(tasks)
            new_mass = sb_add(sb_pv(tasks, zs, sb_cumsum(tasks, zs)), rows, mass)
            return (j - 1, min_mass(new_mass), *new_mass)

        lax.while_loop(live, step, (j_first, lowest, *mass))

    qb = head_queries(qb_ref)
    cfc = cfc_ref[pl.ds(diag_off, TQ), :]
    fox_heads = [first_head + h for h in heads]
    cf_t = [jnp.broadcast_to(jnp.sum(jnp.where(lane == hd, cfc, 0.0), axis=1, keepdims=True), (TQ, LANES))
            for hd in fox_heads]

    def fox_qk(off, width, h, rows):
        return _qk(qb[h][rows], kb_ref[pl.ds(off, width), pair_lanes(h)])

    def fox_logits(z, off, width, h, rows):
        cf_s = cfr_ref[0, pl.ds(fox_heads[h], 1), pl.ds(off, width)]
        return jnp.concatenate([zc + cf_t[h][rows] for zc in _lane_chunks(z)], axis=1) - cf_s

    def chunk_max(s):
        return functools.reduce(jnp.maximum, _lane_chunks(s))

    def fox_probs(off, width, h, rows):
        s = s_ref[h, rows, pl.ds(off, width)]
        m = m_ref[h, rows, :]
        return jnp.concatenate([jnp.exp2(sc - m) for sc in _lane_chunks(s)], axis=1).astype(BF16)

    def fox_pv(p, off, width, h):
        v1 = jnp.concatenate([vb_ref[pl.ds(off, width), pair_lanes(h)], jnp.ones((width, LANES), BF16)], axis=1)
        o = jnp.dot(p, v1, preferred_element_type=F32)
        return o[:, :LANES], o[:, LANES:]

    diag_tasks = [t for h in heads for t in ((2 * i + 1, h, hi_rows, True), (2 * i, h, hi_rows, False),
                                             (2 * i, h, lo_rows, True))]
    zs = sb_qk(diag_tasks)
    fz = [(fox_qk(diag_off, TQ, h, hi_rows), fox_qk(diag_off, TK, h, lo_rows)) for h in heads]
    cs = sb_cumsum(diag_tasks, zs)
    lowest_hi = min_mass([cs[3 * h][:, 0:1] + cs[3 * h + 1][:, 0:1] for h in heads])
    for h in heads:
        s_hi = fox_logits(fz[h][0], diag_off, TQ, h, hi_rows)
        s_hi = jnp.concatenate([s_hi[:, :TK], jnp.where(causal, s_hi[:, TK:], NEG_BIG)], axis=1)
        s_lo = jnp.where(causal, fox_logits(fz[h][1], diag_off, TK, h, lo_rows), NEG_BIG)
        s_ref[h, hi_rows, pl.ds(diag_off, TQ)] = s_hi
        s_ref[h, lo_rows, pl.ds(diag_off, TK)] = s_lo
        m_ref[h, hi_rows, :] = chunk_max(s_hi)
        m_ref[h, lo_rows, :] = chunk_max(s_lo)
    res = sb_pv(diag_tasks, zs, cs)
    mass_lo, mass_hi = [], []
    for h in heads:
        (o_b, r_b), (o_a, r_a), (o_l, r_l) = res[3 * h:3 * h + 3]
        acc_a[h, hi_rows, :] = o_b + o_a * jnp.exp2(-r_b)
        acc_a[h, lo_rows, :] = o_l
        mass_lo.append(r_l)
        mass_hi.append(r_a + r_b)

    def pass1(jj, _):
        off = pl.multiple_of(jj * TQ, TQ)
        for h in heads:
            s = fox_logits(fox_qk(off, TQ, h, all_rows), off, TQ, h, all_rows)
            s_ref[h, :, pl.ds(off, TQ)] = s
            m_ref[h] = jnp.maximum(m_ref[h], chunk_max(s))
        return 0

    lax.fori_loop(0, i, pass1, 0)

    for h in heads:
        m_ref[h] = jnp.broadcast_to(jnp.max(m_ref[h], axis=1, keepdims=True), (TQ, LANES))
    first_tasks = sb_half_tasks(jnp.maximum(2 * i - 1, 0), lo_rows)
    zs = sb_qk(first_tasks)
    probs = [(fox_probs(diag_off, TQ, h, hi_rows), fox_probs(diag_off, TK, h, lo_rows)) for h in heads]
    for h in heads:
        acc_b[h, hi_rows, :], l_ref[h, hi_rows, :] = fox_pv(probs[h][0], diag_off, TQ, h)
        acc_b[h, lo_rows, :], l_ref[h, lo_rows, :] = fox_pv(probs[h][1], diag_off, TK, h)
    cs = sb_cumsum(first_tasks, zs)
    gate = jnp.where(i > 0, 1.0, 0.0).astype(F32)
    lowest_lo = min_mass([mass_lo[h] + cs[h][:, 0:1] * gate for h in heads])
    res = sb_pv(first_tasks, zs, cs)
    mass_lo = sb_add(res, lo_rows, mass_lo, gate=gate)

    def pass2(jj, _):
        off = pl.multiple_of(jj * TQ, TQ)
        for h in heads:
            o, l = fox_pv(fox_probs(off, TQ, h, all_rows), off, TQ, h)
            acc_b[h] += o
            l_ref[h] += l
        return 0

    lax.fori_loop(0, i, pass2, 0)

    sb_stream(lo_rows, 2 * i - 2, mass_lo, lowest_lo)
    sb_stream(hi_rows, 2 * i - 1, mass_hi, lowest_hi)

    yb = [acc_b[h] / l_ref[h] for h in heads]
    for p in range(PAIRS_PER_STEP):
        h0, h1 = p * HEADS_PER_BLOCK, p * HEADS_PER_BLOCK + 1
        cols = slice(p * LANES, (p + 1) * LANES)
        ya_ref[pl.ds(diag_off, TQ), cols] = jnp.where(lane < HEAD_DIM, acc_a[h0], acc_a[h1]).astype(BF16)
        yb_ref[pl.ds(diag_off, TQ), cols] = jnp.where(lane < HEAD_DIM, yb[h0], yb[h1]).astype(BF16)


def _attn_body(*refs, nq):
    def qblock(i, carry):
        _attn_qblock(i, *refs)
        return carry

    lax.fori_loop(0, nq, qblock, 0)


def _attention(qkv, tri_u, cf_col, cf_row, batch, seq):
    nb = N_HEAD_BLOCKS // PAIRS_PER_STEP
    step_lanes = PAIRS_PER_STEP * LANES

    def seq_spec(col):
        return pl.BlockSpec((seq, step_lanes), lambda b, g: (b, col + g))

    y_spec = seq_spec(0)
    y_shape = jax.ShapeDtypeStruct((batch * seq, WIDTH), BF16)
    head_tile = pltpu.VMEM((HEADS_PER_STEP, TQ, LANES), F32)
    return pl.pallas_call(
        functools.partial(_attn_body, nq=seq // TQ),
        grid=(batch, nb),
        in_specs=[seq_spec(c * nb) for c in range(6)]
        + [_const_spec((TK, TK)),
           pl.BlockSpec((seq, LANES), lambda b, g: (b, 0)),
           pl.BlockSpec((1, N_HEADS, seq), lambda b, g: (b, 0, 0))],
        out_specs=[y_spec, y_spec],
        out_shape=[y_shape, y_shape],
        scratch_shapes=[head_tile, pltpu.VMEM((HEADS_PER_STEP, TQ, seq), F32), head_tile, head_tile, head_tile],
        compiler_params=pltpu.CompilerParams(dimension_semantics=("parallel", "parallel"),
                                             vmem_limit_bytes=VMEM_LIMIT),
        name="attention",
    )(qkv, qkv, qkv, qkv, qkv, qkv, tri_u, cf_col, cf_row)


def _post_body(x_ref, ya_ref, yb_ref, g_ref, wgate_ref, bgate_ref, wua_ref, wub_ref, wout_ref, o_ref):
    x = x_ref[...]
    h = _rms(x, g_ref[...]).astype(BF16)
    ya = ya_ref[...]
    yb = yb_ref[...]
    acc = jnp.zeros(x.shape, F32)
    nchunk = D_MODEL // WIDTH
    for c in range(nchunk):
        sl_a = slice(c * WIDTH, (c + 1) * WIDTH)
        sl_b = slice(D_MODEL + c * WIDTH, D_MODEL + (c + 1) * WIDTH)
        ga = jax.nn.sigmoid(jnp.dot(h, wgate_ref[:, sl_a].astype(BF16), preferred_element_type=F32)
                            + bgate_ref[:, sl_a])
        gb = jax.nn.sigmoid(jnp.dot(h, wgate_ref[:, sl_b].astype(BF16), preferred_element_type=F32)
                            + bgate_ref[:, sl_b])
        ua = jnp.dot(ya, wua_ref[:, sl_a].astype(BF16), preferred_element_type=F32)
        ub = jnp.dot(yb, wub_ref[:, sl_a].astype(BF16), preferred_element_type=F32)
        mixed = (ga * ua + gb * ub).astype(BF16)
        acc = acc + jnp.dot(mixed, wout_ref[sl_a, :].astype(BF16), preferred_element_type=F32)
    o_ref[...] = x + acc


def _post(x2d, ya, yb, g, wgate, bgate, wua, wub, wout):
    t = x2d.shape[0]
    tile = pl.BlockSpec((TM_FFN, D_MODEL), lambda i: (i, 0))
    ytile = pl.BlockSpec((TM_FFN, WIDTH), lambda i: (i, 0))
    return pl.pallas_call(
        _post_body,
        grid=(t // TM_FFN,),
        in_specs=[tile, ytile, ytile, _const_spec((1, D_MODEL)), _const_spec((D_MODEL, 2 * D_MODEL)),
                  _const_spec((1, 2 * D_MODEL)), _const_spec((WIDTH, D_MODEL)),
                  _const_spec((WIDTH, D_MODEL)), _const_spec((D_MODEL, D_MODEL))],
        out_specs=tile,
        out_shape=jax.ShapeDtypeStruct((t, D_MODEL), F32),
        compiler_params=pltpu.CompilerParams(dimension_semantics=("parallel",),
                                             vmem_limit_bytes=VMEM_LIMIT),
        name="post",
    )(x2d, ya, yb, g, wgate, bgate, wua, wub, wout)


def kernel(x, norm_ffn1, w_ffn1_gate, w_ffn1_up, w_ffn1_down, norm_mix, w_in, b_forget, w_gate, b_gate,
           w_up_a, w_up_b, w_out, norm_ffn2, w_ffn2_gate, w_ffn2_up, w_ffn2_down, norm_final):
    batch, seq, _ = x.shape
    depth = norm_ffn1.shape[0]
    x2d = x.reshape(batch * seq, D_MODEL)
    gf = norm_final.reshape(1, D_MODEL)

    row = lax.broadcasted_iota(jnp.int32, (TM_PROJ, TM_PROJ), 0)
    col = lax.broadcasted_iota(jnp.int32, (TM_PROJ, TM_PROJ), 1)
    tri_l = (col <= row).astype(BF16)
    rk = lax.broadcasted_iota(jnp.int32, (TK, TK), 0)
    ck = lax.broadcasted_iota(jnp.int32, (TK, TK), 1)
    tri_u = (rk >= ck).astype(BF16)

    for l in range(depth):
        last = l == depth - 1
        x2d = _ffn(x2d, norm_ffn1[l].reshape(1, D_MODEL), w_ffn1_gate[l], w_ffn1_up[l], w_ffn1_down[l], gf,
                   final_norm=False)

        n_f = w_in.shape[2] - 6 * WIDTH
        w_f = jnp.pad(w_in[l][:, 6 * WIDTH:], ((0, 0), (0, LANES - n_f))).astype(BF16)
        bf_p = jnp.pad(b_forget[l], (0, LANES - n_f)).reshape(1, LANES)
        g_mix = norm_mix[l].reshape(1, D_MODEL)
        qkv, cf_col, cf_row = _proj(x2d, g_mix, w_in[l], w_f, bf_p, tri_l, batch, seq)
        y_a, y_b = _attention(qkv, tri_u, cf_col, cf_row, batch, seq)
        x2d = _post(x2d, y_a, y_b, g_mix, w_gate[l], b_gate[l].reshape(1, 2 * D_MODEL), w_up_a[l], w_up_b[l], w_out[l])

        x2d = _ffn(x2d, norm_ffn2[l].reshape(1, D_MODEL), w_ffn2_gate[l], w_ffn2_up[l], w_ffn2_down[l], gf,
                   final_norm=last)
    return x2d.reshape(batch, seq, D_MODEL)
```

```python
import functools

import jax
import jax.numpy as jnp
from jax import lax
from jax.experimental import pallas as pl
from jax.experimental.pallas import tpu as pltpu

D_MODEL = 1024
D_FF = 2816
HEAD_DIM = 64
N_HEADS = 8
WIDTH = N_HEADS * HEAD_DIM
RMS_EPS = 1e-6
ATTN_SCALE = HEAD_DIM ** -0.5
LOG2E = 1.4426950408889634

LANES = 128
HEADS_PER_BLOCK = LANES // HEAD_DIM
N_HEAD_BLOCKS = WIDTH // LANES
PAIRS_PER_STEP = 2
HEADS_PER_STEP = PAIRS_PER_STEP * HEADS_PER_BLOCK

TM_FFN = 1024
TF = 256
TM_PROJ = 512
TK = 256
TQ = 2 * TK
VMEM_LIMIT = 60 * 1024 * 1024

F32 = jnp.float32
BF16 = jnp.bfloat16
NEG_BIG = -1e30
SB_MASS_CUTOFF = 160.0


def _rms(x, g):
    ms = jnp.mean(x * x, axis=-1, keepdims=True)
    return x * lax.rsqrt(ms + RMS_EPS) * g


def _const_spec(shape):
    nd = len(shape)
    return pl.BlockSpec(shape, lambda *_: (0,) * nd, pipeline_mode=pl.Buffered(1))


def _ffn_body(x_ref, g_ref, wg_ref, wu_ref, wd_ref, gf_ref, o_ref, *, final_norm):
    x = x_ref[...]
    xn = _rms(x, g_ref[...]).astype(BF16)
    acc = jnp.zeros(x.shape, F32)
    for c in range(D_FF // TF):
        sl = slice(c * TF, (c + 1) * TF)
        g = jnp.dot(xn, wg_ref[:, sl].astype(BF16), preferred_element_type=F32)
        u = jnp.dot(xn, wu_ref[:, sl].astype(BF16), preferred_element_type=F32)
        h = (g * jax.nn.sigmoid(g) * u).astype(BF16)
        acc = acc + jnp.dot(h, wd_ref[sl, :].astype(BF16), preferred_element_type=F32)
    y = x + 0.5 * acc
    if final_norm:
        y = _rms(y, gf_ref[...])
    o_ref[...] = y


def _ffn(x2d, g, wg, wu, wd, gf, *, final_norm):
    t = x2d.shape[0]
    tile = pl.BlockSpec((TM_FFN, D_MODEL), lambda i: (i, 0))
    return pl.pallas_call(
        functools.partial(_ffn_body, final_norm=final_norm),
        grid=(t // TM_FFN,),
        in_specs=[tile, _const_spec((1, D_MODEL)), _const_spec((D_MODEL, D_FF)),
                  _const_spec((D_MODEL, D_FF)), _const_spec((D_FF, D_MODEL)),
                  _const_spec((1, D_MODEL))],
        out_specs=tile,
        out_shape=jax.ShapeDtypeStruct((t, D_MODEL), F32),
        compiler_params=pltpu.CompilerParams(dimension_semantics=("parallel",),
                                             vmem_limit_bytes=VMEM_LIMIT),
        name="ffn_final" if final_norm else "ffn",
    )(x2d, g, wg, wu, wd, gf)


def _proj_body(x_ref, g_ref, w_ref, wf_ref, bf_ref, tri_ref, qkv_ref, cfc_ref, cfr_ref, carry_ref):
    @pl.when(pl.program_id(1) == 0)
    def _():
        carry_ref[...] = jnp.zeros_like(carry_ref)

    h = _rms(x_ref[...], g_ref[...]).astype(BF16)

    def qkv_chunk(c):
        sl = slice(c * WIDTH, (c + 1) * WIDTH)
        p = jnp.dot(h, w_ref[:, sl].astype(BF16), preferred_element_type=F32)
        if c in (0, 3):
            p = p * (ATTN_SCALE * LOG2E)
        qkv_ref[:, sl] = p.astype(BF16)

    fl = jnp.dot(h, wf_ref[...], preferred_element_type=F32) + bf_ref[...]
    lf = (jnp.minimum(fl, 0.0) - jnp.log1p(jnp.exp(-jnp.abs(fl)))) * LOG2E
    hi = lf.astype(BF16)
    r1 = lf - hi.astype(F32)
    mid = r1.astype(BF16)
    lo = (r1 - mid.astype(F32)).astype(BF16)
    for c in range(3):
        qkv_chunk(c)
    cs = jnp.dot(tri_ref[...], jnp.concatenate([hi, mid, lo], axis=1), preferred_element_type=F32)
    for c in range(3, 6):
        qkv_chunk(c)
    cf = cs[:, :LANES] + cs[:, LANES:2 * LANES] + cs[:, 2 * LANES:] + carry_ref[...]
    carry_ref[...] = cf[TM_PROJ - 1:TM_PROJ, :]
    cfc_ref[...] = cf
    cfr_ref[0] = cf.T[:N_HEADS, :]


def _proj(x2d, g, w, wf, bf, tri, batch, seq):
    t = x2d.shape[0]
    ns = seq // TM_PROJ
    return pl.pallas_call(
        _proj_body,
        grid=(batch, ns),
        in_specs=[pl.BlockSpec((TM_PROJ, D_MODEL), lambda b, s: (b * ns + s, 0)),
                  _const_spec((1, D_MODEL)), _const_spec(w.shape), _const_spec((D_MODEL, LANES)),
                  _const_spec((1, LANES)), _const_spec((TM_PROJ, TM_PROJ))],
        out_specs=[pl.BlockSpec((TM_PROJ, 6 * WIDTH), lambda b, s: (b * ns + s, 0)),
                   pl.BlockSpec((TM_PROJ, LANES), lambda b, s: (b * ns + s, 0)),
                   pl.BlockSpec((1, N_HEADS, TM_PROJ), lambda b, s: (b, 0, s))],
        out_shape=[jax.ShapeDtypeStruct((t, 6 * WIDTH), BF16),
                   jax.ShapeDtypeStruct((t, LANES), F32),
                   jax.ShapeDtypeStruct((batch, N_HEADS, seq), F32)],
        scratch_shapes=[pltpu.VMEM((1, LANES), F32)],
        compiler_params=pltpu.CompilerParams(dimension_semantics=("arbitrary", "arbitrary"),
                                             vmem_limit_bytes=VMEM_LIMIT),
        name="proj",
    )(x2d, g, w, wf, bf, tri)


def _head_split(q):
    lane = lax.broadcasted_iota(jnp.int32, q.shape, 1)
    zero = jnp.zeros_like(q)
    return [jnp.where(lane < HEAD_DIM, q, zero), jnp.where(lane >= HEAD_DIM, q, zero)]


def _qk(qh, kj):
    return lax.dot_general(qh, kj, (((1,), (1,)), ((), ())), preferred_element_type=F32)


def _lane_chunks(x):
    return [x[:, c * LANES:(c + 1) * LANES] for c in range(x.shape[1] // LANES)]


def _aligned(x, m):
    return x if isinstance(x, int) else pl.multiple_of(x, m)


def _attn_qblock(i, qa_ref, ka_ref, va_ref, qb_ref, kb_ref, vb_ref, u_ref, cfc_ref, cfr_ref, ya_ref, yb_ref,
                 acc_a, s_ref, m_ref, l_ref, acc_b):
    first_head = pl.program_id(1) * HEADS_PER_STEP
    heads = range(HEADS_PER_STEP)

    def pair_lanes(h):
        p = h // HEADS_PER_BLOCK
        return slice(p * LANES, (p + 1) * LANES)

    def head_queries(q_ref):
        return [qh for p in range(PAIRS_PER_STEP)
                for qh in _head_split(q_ref[diag_off:diag_off + TQ, p * LANES:(p + 1) * LANES])]
    lo_rows, hi_rows, all_rows = slice(0, TK), slice(TK, TQ), slice(0, TQ)
    t_idx = lax.broadcasted_iota(jnp.int32, (TK, TK), 0)
    s_idx = lax.broadcasted_iota(jnp.int32, (TK, TK), 1)
    strict = s_idx < t_idx
    causal = s_idx <= t_idx
    lane = lax.broadcasted_iota(jnp.int32, (TQ, LANES), 1)
    diag_off = i * TQ
    acc_rows = range(i * HEADS_PER_STEP, (i + 1) * HEADS_PER_STEP)

    def key_off(j):
        return _aligned(j * TK, TK)

    def sb_qk(tasks, qa):
        return [_qk(qa[h][rows], ka_ref[pl.ds(key_off(j), TK), pair_lanes(h)]) for j, h, rows, _ in tasks]

    def sb_cumsum(tasks, zs):
        cs = []
        for z, (_, _, _, diag) in zip(zs, tasks):
            sp = jnp.maximum(z, 0.0) + jnp.log2(1.0 + jnp.exp2(jnp.minimum(z, -z)))
            if diag:
                sp = jnp.where(strict, sp, 0.0)
            cs.append(jnp.dot(sp.astype(BF16), u_ref[...], preferred_element_type=F32))
        return cs

    def sb_pv(tasks, zs, cs):
        outs = []
        for z, c, (j, h, _, diag) in zip(zs, cs, tasks):
            w = jnp.exp2(z - c)
            if diag:
                w = jnp.where(strict, w, 0.0)
            o = jnp.dot(w.astype(BF16), va_ref[pl.ds(key_off(j), TK), pair_lanes(h)], preferred_element_type=F32)
            outs.append((o, c[:, 0:1]))
        return outs

    def sb_half_tasks(j, rows):
        return [(j, h, rows, False) for h in heads]

    def sb_add(res, rows, mass):
        new_mass = []
        for h in heads:
            o, t = res[h]
            acc_a[acc_rows[h], rows, :] += o * jnp.exp2(-mass[h])
            new_mass.append(mass[h] + t)
        return new_mass

    def min_mass(mass):
        return jnp.min(functools.reduce(jnp.minimum, mass))

    def sb_stream(rows, j_first, mass, lowest):
        def live(carry):
            return jnp.logical_and(carry[0] >= 0, carry[1] < SB_MASS_CUTOFF)

        def step(carry):
            j, mass = carry[0], carry[2:]
            tasks = sb_half_tasks(j, rows)
            zs = sb_qk(tasks, head_queries(qa_ref))
            new_mass = sb_add(sb_pv(tasks, zs, sb_cumsum(tasks, zs)), rows, mass)
            return (j - 1, min_mass(new_mass), *new_mass)

        return lambda: lax.while_loop(live, step, (jnp.int32(j_first), lowest, *mass))

    qb = head_queries(qb_ref)
    cfc = cfc_ref[diag_off:diag_off + TQ, :]
    fox_heads = [first_head + h for h in heads]
    cf_t = [jnp.broadcast_to(jnp.sum(jnp.where(lane == hd, cfc, 0.0), axis=1, keepdims=True), (TQ, LANES))
            for hd in fox_heads]

    def fox_qk(off, width, h, rows):
        return _qk(qb[h][rows], kb_ref[pl.ds(off, width), pair_lanes(h)])

    def fox_logits(z, off, width, h, rows):
        cf_s = cfr_ref[0, pl.ds(fox_heads[h], 1), pl.ds(off, width)]
        return jnp.concatenate([zc + cf_t[h][rows] for zc in _lane_chunks(z)], axis=1) - cf_s

    def chunk_max(s):
        return functools.reduce(jnp.maximum, _lane_chunks(s))

    def fox_probs(off, width, h, rows):
        s = s_ref[h, rows, pl.ds(off, width)]
        m = m_ref[h, rows, :]
        return jnp.concatenate([jnp.exp2(sc - m) for sc in _lane_chunks(s)], axis=1).astype(BF16)

    def fox_pv(p, off, width, h):
        v1 = jnp.concatenate([vb_ref[pl.ds(off, width), pair_lanes(h)], jnp.ones((width, LANES), BF16)], axis=1)
        o = jnp.dot(p, v1, preferred_element_type=F32)
        return o[:, :LANES], o[:, LANES:]

    qa = head_queries(qa_ref)
    diag_tasks = [t for h in heads for t in ((2 * i + 1, h, hi_rows, True), (2 * i, h, hi_rows, False),
                                             (2 * i, h, lo_rows, True))]
    zs = sb_qk(diag_tasks, qa)
    fz = [(fox_qk(diag_off, TQ, h, hi_rows), fox_qk(diag_off, TK, h, lo_rows)) for h in heads]
    cs = sb_cumsum(diag_tasks, zs)
    lowest_hi = min_mass([cs[3 * h][:, 0:1] + cs[3 * h + 1][:, 0:1] for h in heads])
    for h in heads:
        s_hi = fox_logits(fz[h][0], diag_off, TQ, h, hi_rows)
        s_hi = jnp.concatenate([s_hi[:, :TK], jnp.where(causal, s_hi[:, TK:], NEG_BIG)], axis=1)
        s_lo = jnp.where(causal, fox_logits(fz[h][1], diag_off, TK, h, lo_rows), NEG_BIG)
        s_ref[h, hi_rows, diag_off:diag_off + TQ] = s_hi
        s_ref[h, lo_rows, diag_off:diag_off + TK] = s_lo
        m_ref[h, hi_rows, :] = chunk_max(s_hi)
        m_ref[h, lo_rows, :] = chunk_max(s_lo)
    res = sb_pv(diag_tasks, zs, cs)
    mass_lo, mass_hi = [], []
    for h in heads:
        (o_b, r_b), (o_a, r_a), (o_l, r_l) = res[3 * h:3 * h + 3]
        acc_a[acc_rows[h], hi_rows, :] = o_b + o_a * jnp.exp2(-r_b)
        acc_a[acc_rows[h], lo_rows, :] = o_l
        mass_lo.append(r_l)
        mass_hi.append(r_a + r_b)

    for off in range(0, diag_off, TQ):
        for h in heads:
            s = fox_logits(fox_qk(off, TQ, h, all_rows), off, TQ, h, all_rows)
            s_ref[h, :, off:off + TQ] = s
            m_ref[h] = jnp.maximum(m_ref[h], chunk_max(s))

    for h in heads:
        m_ref[h] = jnp.broadcast_to(jnp.max(m_ref[h], axis=1, keepdims=True), (TQ, LANES))
    first_tasks = sb_half_tasks(2 * i - 1, lo_rows) if i > 0 else []
    zs = sb_qk(first_tasks, qa)
    probs = [(fox_probs(diag_off, TQ, h, hi_rows), fox_probs(diag_off, TK, h, lo_rows)) for h in heads]
    for h in heads:
        acc_b[h, hi_rows, :], l_ref[h, hi_rows, :] = fox_pv(probs[h][0], diag_off, TQ, h)
        acc_b[h, lo_rows, :], l_ref[h, lo_rows, :] = fox_pv(probs[h][1], diag_off, TK, h)
    if first_tasks:
        cs = sb_cumsum(first_tasks, zs)
        lowest_lo = min_mass([mass_lo[h] + cs[h][:, 0:1] for h in heads])
        mass_lo = sb_add(sb_pv(first_tasks, zs, cs), lo_rows, mass_lo)
    else:
        lowest_lo = min_mass(mass_lo)

    for off in range(0, diag_off, TQ):
        for h in heads:
            o, l = fox_pv(fox_probs(off, TQ, h, all_rows), off, TQ, h)
            acc_b[h] += o
            l_ref[h] += l

    yb = [acc_b[h] / l_ref[h] for h in heads]
    for p in range(PAIRS_PER_STEP):
        h0, h1 = p * HEADS_PER_BLOCK, p * HEADS_PER_BLOCK + 1
        yb_ref[diag_off:diag_off + TQ, p * LANES:(p + 1) * LANES] = (
            jnp.where(lane < HEAD_DIM, yb[h0], yb[h1]).astype(BF16))

    return [sb_stream(lo_rows, 2 * i - 2, mass_lo, lowest_lo), sb_stream(hi_rows, 2 * i - 1, mass_hi, lowest_hi)]


def _attn_body(*refs, nq):
    streams = [run for i in range(nq) for run in _attn_qblock(i, *refs)]
    for run in streams:
        run()
    ya_ref, acc_a = refs[9], refs[11]
    lane = lax.broadcasted_iota(jnp.int32, (TQ, LANES), 1)
    for i in range(nq):
        for p in range(PAIRS_PER_STEP):
            h0 = i * HEADS_PER_STEP + p * HEADS_PER_BLOCK
            ya_ref[i * TQ:(i + 1) * TQ, p * LANES:(p + 1) * LANES] = (
                jnp.where(lane < HEAD_DIM, acc_a[h0], acc_a[h0 + 1]).astype(BF16))


def _attention(qkv, tri_u, cf_col, cf_row, batch, seq):
    nb = N_HEAD_BLOCKS // PAIRS_PER_STEP
    step_lanes = PAIRS_PER_STEP * LANES

    def seq_spec(col):
        return pl.BlockSpec((seq, step_lanes), lambda b, g: (b, col + g))

    y_spec = seq_spec(0)
    y_shape = jax.ShapeDtypeStruct((batch * seq, WIDTH), BF16)
    head_tile = pltpu.VMEM((HEADS_PER_STEP, TQ, LANES), F32)
    nq = seq // TQ
    return pl.pallas_call(
        functools.partial(_attn_body, nq=nq),
        grid=(batch, nb),
        in_specs=[seq_spec(c * nb) for c in range(6)]
        + [_const_spec((TK, TK)),
           pl.BlockSpec((seq, LANES), lambda b, g: (b, 0)),
           pl.BlockSpec((1, N_HEADS, seq), lambda b, g: (b, 0, 0))],
        out_specs=[y_spec, y_spec],
        out_shape=[y_shape, y_shape],
        scratch_shapes=[pltpu.VMEM((nq * HEADS_PER_STEP, TQ, LANES), F32),
                        pltpu.VMEM((HEADS_PER_STEP, TQ, seq), F32), head_tile, head_tile, head_tile],
        compiler_params=pltpu.CompilerParams(dimension_semantics=("parallel", "parallel"),
                                             vmem_limit_bytes=VMEM_LIMIT),
        name="attention",
    )(qkv, qkv, qkv, qkv, qkv, qkv, tri_u, cf_col, cf_row)


def _post_body(x_ref, ya_ref, yb_ref, g_ref, wgate_ref, bgate_ref, wua_ref, wub_ref, wout_ref, o_ref):
    x = x_ref[...]
    h = _rms(x, g_ref[...]).astype(BF16)
    ya = ya_ref[...]
    yb = yb_ref[...]
    acc = jnp.zeros(x.shape, F32)
    nchunk = D_MODEL // WIDTH
    for c in range(nchunk):
        sl_a = slice(c * WIDTH, (c + 1) * WIDTH)
        sl_b = slice(D_MODEL + c * WIDTH, D_MODEL + (c + 1) * WIDTH)
        ga = jax.nn.sigmoid(jnp.dot(h, wgate_ref[:, sl_a].astype(BF16), preferred_element_type=F32)
                            + bgate_ref[:, sl_a])
        gb = jax.nn.sigmoid(jnp.dot(h, wgate_ref[:, sl_b].astype(BF16), preferred_element_type=F32)
                            + bgate_ref[:, sl_b])
        ua = jnp.dot(ya, wua_ref[:, sl_a].astype(BF16), preferred_element_type=F32)
        ub = jnp.dot(yb, wub_ref[:, sl_a].astype(BF16), preferred_element_type=F32)
        mixed = (ga * ua + gb * ub).astype(BF16)
        acc = acc + jnp.dot(mixed, wout_ref[sl_a, :].astype(BF16), preferred_element_type=F32)
    o_ref[...] = x + acc


def _post(x2d, ya, yb, g, wgate, bgate, wua, wub, wout):
    t = x2d.shape[0]
    tile = pl.BlockSpec((TM_FFN, D_MODEL), lambda i: (i, 0))
    ytile = pl.BlockSpec((TM_FFN, WIDTH), lambda i: (i, 0))
    return pl.pallas_call(
        _post_body,
        grid=(t // TM_FFN,),
        in_specs=[tile, ytile, ytile, _const_spec((1, D_MODEL)), _const_spec((D_MODEL, 2 * D_MODEL)),
                  _const_spec((1, 2 * D_MODEL)), _const_spec((WIDTH, D_MODEL)),
                  _const_spec((WIDTH, D_MODEL)), _const_spec((D_MODEL, D_MODEL))],
        out_specs=tile,
        out_shape=jax.ShapeDtypeStruct((t, D_MODEL), F32),
        compiler_params=pltpu.CompilerParams(dimension_semantics=("parallel",),
                                             vmem_limit_bytes=VMEM_LIMIT),
        name="post",
    )(x2d, ya, yb, g, wgate, bgate, wua, wub, wout)


def kernel(x, norm_ffn1, w_ffn1_gate, w_ffn1_up, w_ffn1_down, norm_mix, w_in, b_forget, w_gate, b_gate,
           w_up_a, w_up_b, w_out, norm_ffn2, w_ffn2_gate, w_ffn2_up, w_ffn2_down, norm_final):
    batch, seq, _ = x.shape
    depth = norm_ffn1.shape[0]
    x2d = x.reshape(batch * seq, D_MODEL)
    gf = norm_final.reshape(1, D_MODEL)

    row = lax.broadcasted_iota(jnp.int32, (TM_PROJ, TM_PROJ), 0)
    col = lax.broadcasted_iota(jnp.int32, (TM_PROJ, TM_PROJ), 1)
    tri_l = (col <= row).astype(BF16)
    rk = lax.broadcasted_iota(jnp.int32, (TK, TK), 0)
    ck = lax.broadcasted_iota(jnp.int32, (TK, TK), 1)
    tri_u = (rk >= ck).astype(BF16)

    for l in range(depth):
        last = l == depth - 1
        x2d = _ffn(x2d, norm_ffn1[l].reshape(1, D_MODEL), w_ffn1_gate[l], w_ffn1_up[l], w_ffn1_down[l], gf,
                   final_norm=False)

        n_f = w_in.shape[2] - 6 * WIDTH
        w_f = jnp.pad(w_in[l][:, 6 * WIDTH:], ((0, 0), (0, LANES - n_f))).astype(BF16)
        bf_p = jnp.pad(b_forget[l], (0, LANES - n_f)).reshape(1, LANES)
        g_mix = norm_mix[l].reshape(1, D_MODEL)
        qkv, cf_col, cf_row = _proj(x2d, g_mix, w_in[l], w_f, bf_p, tri_l, batch, seq)
        y_a, y_b = _attention(qkv, tri_u, cf_col, cf_row, batch, seq)
        x2d = _post(x2d, y_a, y_b, g_mix, w_gate[l], b_gate[l].reshape(1, 2 * D_MODEL), w_up_a[l], w_up_b[l], w_out[l])

        x2d = _ffn(x2d, norm_ffn2[l].reshape(1, D_MODEL), w_ffn2_gate[l], w_ffn2_up[l], w_ffn2_down[l], gf,
                   final_norm=last)
    return x2d.reshape(batch, seq, D_MODEL)
```

```python
import functools

import jax
import jax.numpy as jnp
from jax import lax
from jax.experimental import pallas as pl
from jax.experimental.pallas import tpu as pltpu

D_MODEL = 1024
D_FF = 2816
HEAD_DIM = 64
N_HEADS = 8
WIDTH = N_HEADS * HEAD_DIM
RMS_EPS = 1e-6
ATTN_SCALE = HEAD_DIM ** -0.5
LOG2E = 1.4426950408889634

LANES = 128
HEADS_PER_BLOCK = LANES // HEAD_DIM
N_HEAD_BLOCKS = WIDTH // LANES
PAIRS_PER_STEP = 2
HEADS_PER_STEP = PAIRS_PER_STEP * HEADS_PER_BLOCK

TM_FFN = 1024
TF = 256
TM_PROJ = 512
TK = 256
TQ = 2 * TK
VMEM_LIMIT = 60 * 1024 * 1024

F32 = jnp.float32
BF16 = jnp.bfloat16
NEG_BIG = -1e30
_CONTRACT_LAST = (((1,), (1,)), ((), ()))
SB_MASS_CUTOFF = 160.0


def _rms(x, g):
    ms = jnp.mean(x * x, axis=-1, keepdims=True)
    return x * lax.rsqrt(ms + RMS_EPS) * g


def _const_spec(shape):
    nd = len(shape)
    return pl.BlockSpec(shape, lambda *_: (0,) * nd, pipeline_mode=pl.Buffered(1))


def _ffn_body(x_ref, g_ref, wg_ref, wu_ref, wd_ref, gf_ref, o_ref, *, final_norm):
    x = x_ref[...]
    xn = _rms(x, g_ref[...]).astype(BF16)
    acc = jnp.zeros(x.shape, F32)
    for c in range(D_FF // TF):
        sl = slice(c * TF, (c + 1) * TF)
        g = jnp.dot(xn, wg_ref[:, sl].astype(BF16), preferred_element_type=F32)
        u = jnp.dot(xn, wu_ref[:, sl].astype(BF16), preferred_element_type=F32)
        h = (g * jax.nn.sigmoid(g) * u).astype(BF16)
        acc = acc + jnp.dot(h, wd_ref[sl, :].astype(BF16), preferred_element_type=F32)
    y = x + 0.5 * acc
    if final_norm:
        y = _rms(y, gf_ref[...])
    o_ref[...] = y


def _ffn(x2d, g, wg, wu, wd, gf, *, final_norm):
    t = x2d.shape[0]
    tile = pl.BlockSpec((TM_FFN, D_MODEL), lambda i: (i, 0))
    return pl.pallas_call(
        functools.partial(_ffn_body, final_norm=final_norm),
        grid=(t // TM_FFN,),
        in_specs=[tile, _const_spec((1, D_MODEL)), _const_spec((D_MODEL, D_FF)),
                  _const_spec((D_MODEL, D_FF)), _const_spec((D_FF, D_MODEL)),
                  _const_spec((1, D_MODEL))],
        out_specs=tile,
        out_shape=jax.ShapeDtypeStruct((t, D_MODEL), F32),
        compiler_params=pltpu.CompilerParams(dimension_semantics=("parallel",),
                                             vmem_limit_bytes=VMEM_LIMIT),
        name="ffn_final" if final_norm else "ffn",
    )(x2d, g, wg, wu, wd, gf)


def _proj_body(x_ref, g_ref, w_ref, wf_ref, bf_ref, tri_ref, qkv_ref, cfc_ref, cfr_ref, carry_ref):
    @pl.when(pl.program_id(1) == 0)
    def _():
        carry_ref[...] = jnp.zeros_like(carry_ref)

    h = _rms(x_ref[...], g_ref[...]).astype(BF16)

    def qkv_chunk(c):
        sl = slice(c * WIDTH, (c + 1) * WIDTH)
        p = lax.dot_general(h, w_ref[0, sl, :].astype(BF16), _CONTRACT_LAST, preferred_element_type=F32)
        if c in (0, 3):
            p = p * (ATTN_SCALE * LOG2E)
        qkv_ref[:, sl] = p.astype(BF16)

    fl = lax.dot_general(h, wf_ref[...], _CONTRACT_LAST, preferred_element_type=F32) + bf_ref[...]
    lf = (jnp.minimum(fl, 0.0) - jnp.log1p(jnp.exp(-jnp.abs(fl)))) * LOG2E
    hi = lf.astype(BF16)
    r1 = lf - hi.astype(F32)
    mid = r1.astype(BF16)
    lo = (r1 - mid.astype(F32)).astype(BF16)
    for c in range(3):
        qkv_chunk(c)
    cs = jnp.dot(tri_ref[...], jnp.concatenate([hi, mid, lo], axis=1), preferred_element_type=F32)
    for c in range(3, 6):
        qkv_chunk(c)
    cf = cs[:, :LANES] + cs[:, LANES:2 * LANES] + cs[:, 2 * LANES:] + carry_ref[...]
    carry_ref[...] = cf[TM_PROJ - 1:TM_PROJ, :]
    cfc_ref[...] = cf
    cfr_ref[0] = cf.T[:N_HEADS, :]


def _proj(x2d, g, w_t, layer, wf_t, bf, tri, batch, seq):
    t = x2d.shape[0]
    ns = seq // TM_PROJ
    return pl.pallas_call(
        _proj_body,
        grid=(batch, ns),
        in_specs=[pl.BlockSpec((TM_PROJ, D_MODEL), lambda b, s: (b * ns + s, 0)),
                  _const_spec((1, D_MODEL)),
                  pl.BlockSpec((1,) + w_t.shape[1:], lambda b, s: (layer, 0, 0), pipeline_mode=pl.Buffered(1)),
                  _const_spec((LANES, D_MODEL)),
                  _const_spec((1, LANES)), _const_spec((TM_PROJ, TM_PROJ))],
        out_specs=[pl.BlockSpec((TM_PROJ, 6 * WIDTH), lambda b, s: (b * ns + s, 0)),
                   pl.BlockSpec((TM_PROJ, LANES), lambda b, s: (b * ns + s, 0)),
                   pl.BlockSpec((1, N_HEADS, TM_PROJ), lambda b, s: (b, 0, s))],
        out_shape=[jax.ShapeDtypeStruct((t, 6 * WIDTH), BF16),
                   jax.ShapeDtypeStruct((t, LANES), F32),
                   jax.ShapeDtypeStruct((batch, N_HEADS, seq), F32)],
        scratch_shapes=[pltpu.VMEM((1, LANES), F32)],
        compiler_params=pltpu.CompilerParams(dimension_semantics=("arbitrary", "arbitrary"),
                                             vmem_limit_bytes=VMEM_LIMIT),
        name="proj",
    )(x2d, g, w_t, wf_t, bf, tri)


def _head_split(q):
    lane = lax.broadcasted_iota(jnp.int32, q.shape, 1)
    zero = jnp.zeros_like(q)
    return [jnp.where(lane < HEAD_DIM, q, zero), jnp.where(lane >= HEAD_DIM, q, zero)]


def _qk(qh, kj):
    return lax.dot_general(qh, kj, _CONTRACT_LAST, preferred_element_type=F32)


def _lane_chunks(x):
    return [x[:, c * LANES:(c + 1) * LANES] for c in range(x.shape[1] // LANES)]


def _aligned(x, m):
    return x if isinstance(x, int) else pl.multiple_of(x, m)


def _attn_qblock(i, qa_ref, ka_ref, va_ref, qb_ref, kb_ref, vb_ref, u_ref, cfc_ref, cfr_ref, ya_ref, yb_ref,
                 acc_a, s_ref, m_ref, l_ref, acc_b):
    first_head = pl.program_id(1) * HEADS_PER_STEP
    heads = range(HEADS_PER_STEP)

    def pair_lanes(h):
        p = h // HEADS_PER_BLOCK
        return slice(p * LANES, (p + 1) * LANES)

    def head_queries(q_ref):
        return [qh for p in range(PAIRS_PER_STEP)
                for qh in _head_split(q_ref[diag_off:diag_off + TQ, p * LANES:(p + 1) * LANES])]
    lo_rows, hi_rows, all_rows = slice(0, TK), slice(TK, TQ), slice(0, TQ)
    t_idx = lax.broadcasted_iota(jnp.int32, (TK, TK), 0)
    s_idx = lax.broadcasted_iota(jnp.int32, (TK, TK), 1)
    strict = s_idx < t_idx
    causal = s_idx <= t_idx
    lane = lax.broadcasted_iota(jnp.int32, (TQ, LANES), 1)
    diag_off = i * TQ
    acc_rows = range(i * HEADS_PER_STEP, (i + 1) * HEADS_PER_STEP)

    def key_off(j):
        return _aligned(j * TK, TK)

    def sb_qk(tasks, qa):
        return [_qk(qa[h][rows], ka_ref[pl.ds(key_off(j), TK), pair_lanes(h)]) for j, h, rows, _ in tasks]

    def sb_cumsum(tasks, zs):
        cs = []
        for z, (_, _, _, diag) in zip(zs, tasks):
            sp = jnp.maximum(z, 0.0) + jnp.log2(1.0 + jnp.exp2(jnp.minimum(z, -z)))
            if diag:
                sp = jnp.where(strict, sp, 0.0)
            cs.append(jnp.dot(sp.astype(BF16), u_ref[...], preferred_element_type=F32))
        return cs

    def sb_pv(tasks, zs, cs):
        outs = []
        for z, c, (j, h, _, diag) in zip(zs, cs, tasks):
            w = jnp.exp2(z - c)
            if diag:
                w = jnp.where(strict, w, 0.0)
            o = jnp.dot(w.astype(BF16), va_ref[pl.ds(key_off(j), TK), pair_lanes(h)], preferred_element_type=F32)
            outs.append((o, c[:, 0:1]))
        return outs

    def sb_half_tasks(j, rows):
        return [(j, h, rows, False) for h in heads]

    def sb_add(res, rows, mass):
        new_mass = []
        for h in heads:
            o, t = res[h]
            acc_a[acc_rows[h], rows, :] += o * jnp.exp2(-mass[h])
            new_mass.append(mass[h] + t)
        return new_mass

    def min_mass(mass):
        return jnp.min(functools.reduce(jnp.minimum, mass))

    def sb_stream(rows, j_first, mass, lowest):
        def live(carry):
            return jnp.logical_and(carry[0] >= 0, carry[1] < SB_MASS_CUTOFF)

        def step(carry):
            j, mass = carry[0], carry[2:]
            tasks = sb_half_tasks(j, rows)
            zs = sb_qk(tasks, head_queries(qa_ref))
            new_mass = sb_add(sb_pv(tasks, zs, sb_cumsum(tasks, zs)), rows, mass)
            return (j - 1, min_mass(new_mass), *new_mass)

        return lambda: lax.while_loop(live, step, (jnp.int32(j_first), lowest, *mass))

    qb = head_queries(qb_ref)
    cfc = cfc_ref[diag_off:diag_off + TQ, :]
    fox_heads = [first_head + h for h in heads]
    cf_t = [jnp.broadcast_to(jnp.sum(jnp.where(lane == hd, cfc, 0.0), axis=1, keepdims=True), (TQ, LANES))
            for hd in fox_heads]

    def fox_qk(off, width, h, rows):
        return _qk(qb[h][rows], kb_ref[pl.ds(off, width), pair_lanes(h)])

    def fox_logits(z, off, width, h, rows):
        cf_s = cfr_ref[0, pl.ds(fox_heads[h], 1), pl.ds(off, width)]
        return jnp.concatenate([zc + cf_t[h][rows] for zc in _lane_chunks(z)], axis=1) - cf_s

    def chunk_max(s):
        return functools.reduce(jnp.maximum, _lane_chunks(s))

    def fox_probs(off, width, h, rows):
        s = s_ref[h, rows, pl.ds(off, width)]
        m = m_ref[h, rows, :]
        return jnp.concatenate([jnp.exp2(sc - m) for sc in _lane_chunks(s)], axis=1).astype(BF16)

    def fox_pv(p, off, width, h):
        v1 = jnp.concatenate([vb_ref[pl.ds(off, width), pair_lanes(h)], jnp.ones((width, LANES), BF16)], axis=1)
        o = jnp.dot(p, v1, preferred_element_type=F32)
        return o[:, :LANES], o[:, LANES:]

    qa = head_queries(qa_ref)
    diag_tasks = [t for h in heads for t in ((2 * i + 1, h, hi_rows, True), (2 * i, h, hi_rows, False),
                                             (2 * i, h, lo_rows, True))]
    zs = sb_qk(diag_tasks, qa)
    fz = [(fox_qk(diag_off, TQ, h, hi_rows), fox_qk(diag_off, TK, h, lo_rows)) for h in heads]
    cs = sb_cumsum(diag_tasks, zs)
    lowest_hi = min_mass([cs[3 * h][:, 0:1] + cs[3 * h + 1][:, 0:1] for h in heads])
    for h in heads:
        s_hi = fox_logits(fz[h][0], diag_off, TQ, h, hi_rows)
        s_hi = jnp.concatenate([s_hi[:, :TK], jnp.where(causal, s_hi[:, TK:], NEG_BIG)], axis=1)
        s_lo = jnp.where(causal, fox_logits(fz[h][1], diag_off, TK, h, lo_rows), NEG_BIG)
        s_ref[h, hi_rows, diag_off:diag_off + TQ] = s_hi
        s_ref[h, lo_rows, diag_off:diag_off + TK] = s_lo
        m_ref[h, hi_rows, :] = chunk_max(s_hi)
        m_ref[h, lo_rows, :] = chunk_max(s_lo)
    res = sb_pv(diag_tasks, zs, cs)
    mass_lo, mass_hi = [], []
    for h in heads:
        (o_b, r_b), (o_a, r_a), (o_l, r_l) = res[3 * h:3 * h + 3]
        acc_a[acc_rows[h], hi_rows, :] = o_b + o_a * jnp.exp2(-r_b)
        acc_a[acc_rows[h], lo_rows, :] = o_l
        mass_lo.append(r_l)
        mass_hi.append(r_a + r_b)

    for off in range(0, diag_off, TQ):
        for h in heads:
            s = fox_logits(fox_qk(off, TQ, h, all_rows), off, TQ, h, all_rows)
            s_ref[h, :, off:off + TQ] = s
            m_ref[h] = jnp.maximum(m_ref[h], chunk_max(s))

    for h in heads:
        m_ref[h] = jnp.broadcast_to(jnp.max(m_ref[h], axis=1, keepdims=True), (TQ, LANES))
    first_tasks = sb_half_tasks(2 * i - 1, lo_rows) if i > 0 else []
    zs = sb_qk(first_tasks, qa)
    probs = [(fox_probs(diag_off, TQ, h, hi_rows), fox_probs(diag_off, TK, h, lo_rows)) for h in heads]
    for h in heads:
        acc_b[h, hi_rows, :], l_ref[h, hi_rows, :] = fox_pv(probs[h][0], diag_off, TQ, h)
        acc_b[h, lo_rows, :], l_ref[h, lo_rows, :] = fox_pv(probs[h][1], diag_off, TK, h)
    if first_tasks:
        cs = sb_cumsum(first_tasks, zs)
        lowest_lo = min_mass([mass_lo[h] + cs[h][:, 0:1] for h in heads])
        mass_lo = sb_add(sb_pv(first_tasks, zs, cs), lo_rows, mass_lo)
    else:
        lowest_lo = min_mass(mass_lo)

    for off in range(0, diag_off, TQ):
        for h in heads:
            o, l = fox_pv(fox_probs(off, TQ, h, all_rows), off, TQ, h)
            acc_b[h] += o
            l_ref[h] += l

    yb = [acc_b[h] / l_ref[h] for h in heads]
    for p in range(PAIRS_PER_STEP):
        h0, h1 = p * HEADS_PER_BLOCK, p * HEADS_PER_BLOCK + 1
        yb_ref[diag_off:diag_off + TQ, p * LANES:(p + 1) * LANES] = (
            jnp.where(lane < HEAD_DIM, yb[h0], yb[h1]).astype(BF16))

    return [sb_stream(lo_rows, 2 * i - 2, mass_lo, lowest_lo), sb_stream(hi_rows, 2 * i - 1, mass_hi, lowest_hi)]


def _attn_body(*refs, nq):
    streams = [run for i in range(nq) for run in _attn_qblock(i, *refs)]
    for run in streams:
        run()
    ya_ref, acc_a = refs[9], refs[11]
    lane = lax.broadcasted_iota(jnp.int32, (TQ, LANES), 1)
    for i in range(nq):
        for p in range(PAIRS_PER_STEP):
            h0 = i * HEADS_PER_STEP + p * HEADS_PER_BLOCK
            ya_ref[i * TQ:(i + 1) * TQ, p * LANES:(p + 1) * LANES] = (
                jnp.where(lane < HEAD_DIM, acc_a[h0], acc_a[h0 + 1]).astype(BF16))


def _attention(qkv, tri_u, cf_col, cf_row, batch, seq):
    nb = N_HEAD_BLOCKS // PAIRS_PER_STEP
    step_lanes = PAIRS_PER_STEP * LANES

    def seq_spec(col):
        return pl.BlockSpec((seq, step_lanes), lambda b, g: (b, col + g))

    y_spec = seq_spec(0)
    y_shape = jax.ShapeDtypeStruct((batch * seq, WIDTH), BF16)
    head_tile = pltpu.VMEM((HEADS_PER_STEP, TQ, LANES), F32)
    nq = seq // TQ
    return pl.pallas_call(
        functools.partial(_attn_body, nq=nq),
        grid=(batch, nb),
        in_specs=[seq_spec(c * nb) for c in range(6)]
        + [_const_spec((TK, TK)),
           pl.BlockSpec((seq, LANES), lambda b, g: (b, 0)),
           pl.BlockSpec((1, N_HEADS, seq), lambda b, g: (b, 0, 0))],
        out_specs=[y_spec, y_spec],
        out_shape=[y_shape, y_shape],
        scratch_shapes=[pltpu.VMEM((nq * HEADS_PER_STEP, TQ, LANES), F32),
                        pltpu.VMEM((HEADS_PER_STEP, TQ, seq), F32), head_tile, head_tile, head_tile],
        compiler_params=pltpu.CompilerParams(dimension_semantics=("parallel", "parallel"),
                                             vmem_limit_bytes=VMEM_LIMIT),
        name="attention",
    )(qkv, qkv, qkv, qkv, qkv, qkv, tri_u, cf_col, cf_row)


def _post_body(x_ref, ya_ref, yb_ref, g_ref, wgate_ref, bgate_ref, wua_ref, wub_ref, wout_ref, o_ref):
    x = x_ref[...]
    h = _rms(x, g_ref[...]).astype(BF16)
    ya = ya_ref[...]
    yb = yb_ref[...]
    acc = jnp.zeros(x.shape, F32)
    nchunk = D_MODEL // WIDTH
    for c in range(nchunk):
        sl_a = slice(c * WIDTH, (c + 1) * WIDTH)
        sl_b = slice(D_MODEL + c * WIDTH, D_MODEL + (c + 1) * WIDTH)
        ga = jax.nn.sigmoid(jnp.dot(h, wgate_ref[:, sl_a].astype(BF16), preferred_element_type=F32)
                            + bgate_ref[:, sl_a])
        gb = jax.nn.sigmoid(jnp.dot(h, wgate_ref[:, sl_b].astype(BF16), preferred_element_type=F32)
                            + bgate_ref[:, sl_b])
        ua = jnp.dot(ya, wua_ref[:, sl_a].astype(BF16), preferred_element_type=F32)
        ub = jnp.dot(yb, wub_ref[:, sl_a].astype(BF16), preferred_element_type=F32)
        mixed = (ga * ua + gb * ub).astype(BF16)
        acc = acc + jnp.dot(mixed, wout_ref[sl_a, :].astype(BF16), preferred_element_type=F32)
    o_ref[...] = x + acc


def _post(x2d, ya, yb, g, wgate, bgate, wua, wub, wout):
    t = x2d.shape[0]
    tile = pl.BlockSpec((TM_FFN, D_MODEL), lambda i: (i, 0))
    ytile = pl.BlockSpec((TM_FFN, WIDTH), lambda i: (i, 0))
    return pl.pallas_call(
        _post_body,
        grid=(t // TM_FFN,),
        in_specs=[tile, ytile, ytile, _const_spec((1, D_MODEL)), _const_spec((D_MODEL, 2 * D_MODEL)),
                  _const_spec((1, 2 * D_MODEL)), _const_spec((WIDTH, D_MODEL)),
                  _const_spec((WIDTH, D_MODEL)), _const_spec((D_MODEL, D_MODEL))],
        out_specs=tile,
        out_shape=jax.ShapeDtypeStruct((t, D_MODEL), F32),
        compiler_params=pltpu.CompilerParams(dimension_semantics=("parallel",),
                                             vmem_limit_bytes=VMEM_LIMIT),
        name="post",
    )(x2d, ya, yb, g, wgate, bgate, wua, wub, wout)


def kernel(x, norm_ffn1, w_ffn1_gate, w_ffn1_up, w_ffn1_down, norm_mix, w_in, b_forget, w_gate, b_gate,
           w_up_a, w_up_b, w_out, norm_ffn2, w_ffn2_gate, w_ffn2_up, w_ffn2_down, norm_final):
    batch, seq, _ = x.shape
    depth = norm_ffn1.shape[0]
    x2d = x.reshape(batch * seq, D_MODEL)
    gf = norm_final.reshape(1, D_MODEL)

    row = lax.broadcasted_iota(jnp.int32, (TM_PROJ, TM_PROJ), 0)
    col = lax.broadcasted_iota(jnp.int32, (TM_PROJ, TM_PROJ), 1)
    tri_l = (col <= row).astype(BF16)
    rk = lax.broadcasted_iota(jnp.int32, (TK, TK), 0)
    ck = lax.broadcasted_iota(jnp.int32, (TK, TK), 1)
    tri_u = (rk >= ck).astype(BF16)

    for l in range(depth):
        last = l == depth - 1
        x2d = _ffn(x2d, norm_ffn1[l].reshape(1, D_MODEL), w_ffn1_gate[l], w_ffn1_up[l], w_ffn1_down[l], gf,
                   final_norm=False)

        n_f = w_in.shape[2] - 6 * WIDTH
        w_in_t = jnp.swapaxes(w_in, 1, 2)
        w_f = jnp.pad(w_in_t[l, 6 * WIDTH:, :], ((0, LANES - n_f), (0, 0))).astype(BF16)
        bf_p = jnp.pad(b_forget[l], (0, LANES - n_f)).reshape(1, LANES)
        g_mix = norm_mix[l].reshape(1, D_MODEL)
        qkv, cf_col, cf_row = _proj(x2d, g_mix, w_in_t, l, w_f, bf_p, tri_l, batch, seq)
        y_a, y_b = _attention(qkv, tri_u, cf_col, cf_row, batch, seq)
        x2d = _post(x2d, y_a, y_b, g_mix, w_gate[l], b_gate[l].reshape(1, 2 * D_MODEL), w_up_a[l], w_up_b[l], w_out[l])

        x2d = _ffn(x2d, norm_ffn2[l].reshape(1, D_MODEL), w_ffn2_gate[l], w_ffn2_up[l], w_ffn2_down[l], gf,
                   final_norm=last)
    return x2d.reshape(batch, seq, D_MODEL)
```

```python
import functools

import jax
import jax.numpy as jnp
from jax import lax
from jax.experimental import pallas as pl
from jax.experimental.pallas import tpu as pltpu

D_MODEL = 1024
D_FF = 2816
HEAD_DIM = 64
N_HEADS = 8
WIDTH = N_HEADS * HEAD_DIM
RMS_EPS = 1e-6
ATTN_SCALE = HEAD_DIM ** -0.5
LOG2E = 1.4426950408889634

LANES = 128
HEADS_PER_BLOCK = LANES // HEAD_DIM
N_HEAD_BLOCKS = WIDTH // LANES
PAIRS_PER_STEP = 2
HEADS_PER_STEP = PAIRS_PER_STEP * HEADS_PER_BLOCK

TM_FFN = 1024
TF = 256
TM_PROJ = 512
TK = 256
TQ = 2 * TK
VMEM_LIMIT = 60 * 1024 * 1024

F32 = jnp.float32
BF16 = jnp.bfloat16
NEG_BIG = -1e30
_CONTRACT_LAST = (((1,), (1,)), ((), ()))
SB_MASS_CUTOFF = 160.0


def _rms(x, g):
    ms = jnp.mean(x * x, axis=-1, keepdims=True)
    return x * lax.rsqrt(ms + RMS_EPS) * g


def _const_spec(shape):
    nd = len(shape)
    return pl.BlockSpec(shape, lambda *_: (0,) * nd, pipeline_mode=pl.Buffered(1))


def _ffn_body(x_ref, g_ref, wg_ref, wu_ref, wd_ref, gf_ref, o_ref, *, final_norm):
    x = x_ref[...]
    xn = _rms(x, g_ref[...]).astype(BF16)
    acc = jnp.zeros(x.shape, F32)
    for c in range(D_FF // TF):
        sl = slice(c * TF, (c + 1) * TF)
        g = jnp.dot(xn, wg_ref[:, sl].astype(BF16), preferred_element_type=F32)
        u = jnp.dot(xn, wu_ref[:, sl].astype(BF16), preferred_element_type=F32)
        h = (g * jax.nn.sigmoid(g) * u).astype(BF16)
        acc = acc + jnp.dot(h, wd_ref[sl, :].astype(BF16), preferred_element_type=F32)
    y = x + 0.5 * acc
    if final_norm:
        y = _rms(y, gf_ref[...])
    o_ref[...] = y


def _ffn(x2d, g, wg, wu, wd, gf, *, final_norm):
    t = x2d.shape[0]
    tile = pl.BlockSpec((TM_FFN, D_MODEL), lambda i: (i, 0))
    return pl.pallas_call(
        functools.partial(_ffn_body, final_norm=final_norm),
        grid=(t // TM_FFN,),
        in_specs=[tile, _const_spec((1, D_MODEL)), _const_spec((D_MODEL, D_FF)),
                  _const_spec((D_MODEL, D_FF)), _const_spec((D_FF, D_MODEL)),
                  _const_spec((1, D_MODEL))],
        out_specs=tile,
        out_shape=jax.ShapeDtypeStruct((t, D_MODEL), F32),
        compiler_params=pltpu.CompilerParams(dimension_semantics=("parallel",),
                                             vmem_limit_bytes=VMEM_LIMIT),
        name="ffn_final" if final_norm else "ffn",
    )(x2d, g, wg, wu, wd, gf)


def _proj_body(x_ref, g_ref, w_ref, wf_ref, bf_ref, tri_ref, qkv_ref, cfc_ref, cfr_ref, carry_ref):
    @pl.when(pl.program_id(1) == 0)
    def _():
        carry_ref[...] = jnp.zeros_like(carry_ref)

    h = _rms(x_ref[...], g_ref[...]).astype(BF16)

    def qkv_chunk(c):
        sl = slice(c * WIDTH, (c + 1) * WIDTH)
        p = lax.dot_general(h, w_ref[0, sl, :].astype(BF16), _CONTRACT_LAST, preferred_element_type=F32)
        if c in (0, 3):
            p = p * (ATTN_SCALE * LOG2E)
        qkv_ref[:, sl] = p.astype(BF16)

    fl = lax.dot_general(h, wf_ref[...], _CONTRACT_LAST, preferred_element_type=F32) + bf_ref[...]
    lf = (jnp.minimum(fl, 0.0) - jnp.log1p(jnp.exp(-jnp.abs(fl)))) * LOG2E
    hi = lf.astype(BF16)
    r1 = lf - hi.astype(F32)
    mid = r1.astype(BF16)
    lo = (r1 - mid.astype(F32)).astype(BF16)
    for c in range(3):
        qkv_chunk(c)
    cs = jnp.dot(tri_ref[...], jnp.concatenate([hi, mid, lo], axis=1), preferred_element_type=F32)
    for c in range(3, 6):
        qkv_chunk(c)
    cf = cs[:, :LANES] + cs[:, LANES:2 * LANES] + cs[:, 2 * LANES:] + carry_ref[...]
    carry_ref[...] = cf[TM_PROJ - 1:TM_PROJ, :]
    cfc_ref[...] = cf
    cfr_ref[0] = cf.T[:N_HEADS, :]


def _proj(x2d, g, w_t, layer, wf_t, bf, tri, batch, seq):
    t = x2d.shape[0]
    ns = seq // TM_PROJ
    return pl.pallas_call(
        _proj_body,
        grid=(batch, ns),
        in_specs=[pl.BlockSpec((TM_PROJ, D_MODEL), lambda b, s: (b * ns + s, 0)),
                  _const_spec((1, D_MODEL)),
                  pl.BlockSpec((1,) + w_t.shape[1:], lambda b, s: (layer, 0, 0), pipeline_mode=pl.Buffered(1)),
                  _const_spec((LANES, D_MODEL)),
                  _const_spec((1, LANES)), _const_spec((TM_PROJ, TM_PROJ))],
        out_specs=[pl.BlockSpec((TM_PROJ, 6 * WIDTH), lambda b, s: (b * ns + s, 0)),
                   pl.BlockSpec((TM_PROJ, LANES), lambda b, s: (b * ns + s, 0)),
                   pl.BlockSpec((1, N_HEADS, TM_PROJ), lambda b, s: (b, 0, s))],
        out_shape=[jax.ShapeDtypeStruct((t, 6 * WIDTH), BF16),
                   jax.ShapeDtypeStruct((t, LANES), F32),
                   jax.ShapeDtypeStruct((batch, N_HEADS, seq), F32)],
        scratch_shapes=[pltpu.VMEM((1, LANES), F32)],
        compiler_params=pltpu.CompilerParams(dimension_semantics=("arbitrary", "arbitrary"),
                                             vmem_limit_bytes=VMEM_LIMIT),
        name="proj",
    )(x2d, g, w_t, wf_t, bf, tri)


def _head_split(q):
    lane = lax.broadcasted_iota(jnp.int32, q.shape, 1)
    zero = jnp.zeros_like(q)
    return [jnp.where(lane < HEAD_DIM, q, zero), jnp.where(lane >= HEAD_DIM, q, zero)]


def _qk(qh, kj):
    return lax.dot_general(qh, kj, _CONTRACT_LAST, preferred_element_type=F32)


def _lane_chunks(x):
    return [x[:, c * LANES:(c + 1) * LANES] for c in range(x.shape[1] // LANES)]


def _aligned(x, m):
    return x if isinstance(x, int) else pl.multiple_of(x, m)


def _attn_qblock(i, qa_ref, ka_ref, va_ref, qb_ref, kb_ref, vb_ref, u_ref, cfc_ref, cfr_ref, ya_ref, yb_ref,
                 acc_a, s_ref, m_ref, l_ref, acc_b):
    first_head = pl.program_id(1) * HEADS_PER_STEP
    heads = range(HEADS_PER_STEP)

    def pair_lanes(h):
        p = h // HEADS_PER_BLOCK
        return slice(p * LANES, (p + 1) * LANES)

    def head_queries(q_ref):
        return [qh for p in range(PAIRS_PER_STEP)
                for qh in _head_split(q_ref[diag_off:diag_off + TQ, p * LANES:(p + 1) * LANES])]
    lo_rows, hi_rows, all_rows = slice(0, TK), slice(TK, TQ), slice(0, TQ)
    t_idx = lax.broadcasted_iota(jnp.int32, (TK, TK), 0)
    s_idx = lax.broadcasted_iota(jnp.int32, (TK, TK), 1)
    strict = s_idx < t_idx
    causal = s_idx <= t_idx
    lane = lax.broadcasted_iota(jnp.int32, (TQ, LANES), 1)
    diag_off = i * TQ
    acc_rows = range(i * HEADS_PER_STEP, (i + 1) * HEADS_PER_STEP)

    def key_off(j):
        return _aligned(j * TK, TK)

    def sb_qk(tasks, qa):
        return [_qk(qa[h][rows], ka_ref[pl.ds(key_off(j), TK), pair_lanes(h)]) for j, h, rows, _ in tasks]

    def sb_cumsum(tasks, zs):
        cs = []
        for z, (_, _, _, diag) in zip(zs, tasks):
            sp = jnp.maximum(z, 0.0) + jnp.log2(1.0 + jnp.exp2(jnp.minimum(z, -z)))
            if diag:
                sp = jnp.where(strict, sp, 0.0)
            cs.append(jnp.dot(sp.astype(BF16), u_ref[...], preferred_element_type=F32))
        return cs

    def sb_pv(tasks, zs, cs):
        outs = []
        for z, c, (j, h, _, diag) in zip(zs, cs, tasks):
            w = jnp.exp2(z - c)
            if diag:
                w = jnp.where(strict, w, 0.0)
            o = jnp.dot(w.astype(BF16), va_ref[pl.ds(key_off(j), TK), pair_lanes(h)], preferred_element_type=F32)
            outs.append((o, c[:, 0:1]))
        return outs

    def sb_half_tasks(j, rows):
        return [(j, h, rows, False) for h in heads]

    def sb_add(res, rows, mass):
        new_mass = []
        for h in heads:
            o, t = res[h]
            acc_a[acc_rows[h], rows, :] += o * jnp.exp2(-mass[h])
            new_mass.append(mass[h] + t)
        return new_mass

    def min_mass(mass):
        return jnp.min(functools.reduce(jnp.minimum, mass))

    def sb_stream(rows, j_first, mass, lowest):
        def live(carry):
            return jnp.logical_and(carry[0] >= 0, carry[1] < SB_MASS_CUTOFF)

        def step(carry):
            j, mass = carry[0], carry[2:]
            tasks = sb_half_tasks(j, rows)
            zs = sb_qk(tasks, head_queries(qa_ref))
            new_mass = sb_add(sb_pv(tasks, zs, sb_cumsum(tasks, zs)), rows, mass)
            return (j - 1, min_mass(new_mass), *new_mass)

        return lambda: lax.while_loop(live, step, (jnp.int32(j_first), lowest, *mass))

    qb = head_queries(qb_ref)
    cfc = cfc_ref[diag_off:diag_off + TQ, :]
    fox_heads = [first_head + h for h in heads]
    cf_t = [jnp.broadcast_to(jnp.sum(jnp.where(lane == hd, cfc, 0.0), axis=1, keepdims=True), (TQ, LANES))
            for hd in fox_heads]

    def fox_qk(off, width, h, rows):
        return _qk(qb[h][rows], kb_ref[pl.ds(off, width), pair_lanes(h)])

    def fox_logits(z, off, width, h, rows):
        cf_s = cfr_ref[0, pl.ds(fox_heads[h], 1), pl.ds(off, width)]
        return jnp.concatenate([zc + cf_t[h][rows] for zc in _lane_chunks(z)], axis=1) - cf_s

    def chunk_max(s):
        return functools.reduce(jnp.maximum, _lane_chunks(s))

    def fox_probs(off, width, h, rows):
        s = s_ref[h, rows, pl.ds(off, width)]
        m = m_ref[h, rows, :]
        return jnp.concatenate([jnp.exp2(sc - m) for sc in _lane_chunks(s)], axis=1).astype(BF16)

    def fox_pv(p, off, width, h):
        v1 = jnp.concatenate([vb_ref[pl.ds(off, width), pair_lanes(h)], jnp.ones((width, LANES), BF16)], axis=1)
        o = jnp.dot(p, v1, preferred_element_type=F32)
        return o[:, :LANES], o[:, LANES:]


    qa = head_queries(qa_ref)
    diag_tasks = [t for h in heads for t in ((2 * i + 1, h, hi_rows, True), (2 * i, h, hi_rows, False),
                                             (2 * i, h, lo_rows, True))]
    zs = sb_qk(diag_tasks, qa)
    fz = [(fox_qk(diag_off, TQ, h, hi_rows), fox_qk(diag_off, TK, h, lo_rows)) for h in heads]
    for h in heads:
        s_hi = fox_logits(fz[h][0], diag_off, TQ, h, hi_rows)
        s_hi = jnp.concatenate([s_hi[:, :TK], jnp.where(causal, s_hi[:, TK:], NEG_BIG)], axis=1)
        s_lo = jnp.where(causal, fox_logits(fz[h][1], diag_off, TK, h, lo_rows), NEG_BIG)
        s_ref[h, hi_rows, diag_off:diag_off + TQ] = s_hi
        s_ref[h, lo_rows, diag_off:diag_off + TK] = s_lo
        m_ref[h, hi_rows, :] = chunk_max(s_hi)
        m_ref[h, lo_rows, :] = chunk_max(s_lo)
    for off in range(0, diag_off, TQ):
        for h in heads:
            s = fox_logits(fox_qk(off, TQ, h, all_rows), off, TQ, h, all_rows)
            s_ref[h, :, off:off + TQ] = s
            m_ref[h] = jnp.maximum(m_ref[h], chunk_max(s))
    cs = sb_cumsum(diag_tasks, zs)
    lowest_hi = min_mass([cs[3 * h][:, 0:1] + cs[3 * h + 1][:, 0:1] for h in heads])
    res = sb_pv(diag_tasks, zs, cs)
    mass_lo, mass_hi = [], []
    for h in heads:
        (o_b, r_b), (o_a, r_a), (o_l, r_l) = res[3 * h:3 * h + 3]
        acc_a[acc_rows[h], hi_rows, :] = o_b + o_a * jnp.exp2(-r_b)
        acc_a[acc_rows[h], lo_rows, :] = o_l
        mass_lo.append(r_l)
        mass_hi.append(r_a + r_b)

    for h in heads:
        m_ref[h] = jnp.broadcast_to(jnp.max(m_ref[h], axis=1, keepdims=True), (TQ, LANES))
    first_tasks = sb_half_tasks(2 * i - 1, lo_rows) if i > 0 else []
    zs = sb_qk(first_tasks, qa)
    probs = [(fox_probs(diag_off, TQ, h, hi_rows), fox_probs(diag_off, TK, h, lo_rows)) for h in heads]
    for h in heads:
        acc_b[h, hi_rows, :], l_ref[h, hi_rows, :] = fox_pv(probs[h][0], diag_off, TQ, h)
        acc_b[h, lo_rows, :], l_ref[h, lo_rows, :] = fox_pv(probs[h][1], diag_off, TK, h)
    for off in range(0, diag_off, TQ):
        for h in heads:
            o, l = fox_pv(fox_probs(off, TQ, h, all_rows), off, TQ, h)
            acc_b[h] += o
            l_ref[h] += l
    if first_tasks:
        cs = sb_cumsum(first_tasks, zs)
        lowest_lo = min_mass([mass_lo[h] + cs[h][:, 0:1] for h in heads])
        mass_lo = sb_add(sb_pv(first_tasks, zs, cs), lo_rows, mass_lo)
    else:
        lowest_lo = min_mass(mass_lo)

    yb = [acc_b[h] / l_ref[h] for h in heads]
    for p in range(PAIRS_PER_STEP):
        h0, h1 = p * HEADS_PER_BLOCK, p * HEADS_PER_BLOCK + 1
        yb_ref[diag_off:diag_off + TQ, p * LANES:(p + 1) * LANES] = (
            jnp.where(lane < HEAD_DIM, yb[h0], yb[h1]).astype(BF16))

    return [sb_stream(lo_rows, 2 * i - 2, mass_lo, lowest_lo), sb_stream(hi_rows, 2 * i - 1, mass_hi, lowest_hi)]


def _attn_body(*refs, nq):
    streams = [run for i in range(nq) for run in _attn_qblock(i, *refs)]
    for run in streams:
        run()
    ya_ref, acc_a = refs[9], refs[11]
    lane = lax.broadcasted_iota(jnp.int32, (TQ, LANES), 1)
    for i in range(nq):
        for p in range(PAIRS_PER_STEP):
            h0 = i * HEADS_PER_STEP + p * HEADS_PER_BLOCK
            ya_ref[i * TQ:(i + 1) * TQ, p * LANES:(p + 1) * LANES] = (
                jnp.where(lane < HEAD_DIM, acc_a[h0], acc_a[h0 + 1]).astype(BF16))


def _attention(qkv, tri_u, cf_col, cf_row, batch, seq):
    nb = N_HEAD_BLOCKS // PAIRS_PER_STEP
    step_lanes = PAIRS_PER_STEP * LANES

    def seq_spec(col):
        return pl.BlockSpec((seq, step_lanes), lambda b, g: (b, col + g))

    y_spec = seq_spec(0)
    y_shape = jax.ShapeDtypeStruct((batch * seq, WIDTH), BF16)
    head_tile = pltpu.VMEM((HEADS_PER_STEP, TQ, LANES), F32)
    nq = seq // TQ
    return pl.pallas_call(
        functools.partial(_attn_body, nq=nq),
        grid=(batch, nb),
        in_specs=[seq_spec(c * nb) for c in range(6)]
        + [_const_spec((TK, TK)),
           pl.BlockSpec((seq, LANES), lambda b, g: (b, 0)),
           pl.BlockSpec((1, N_HEADS, seq), lambda b, g: (b, 0, 0))],
        out_specs=[y_spec, y_spec],
        out_shape=[y_shape, y_shape],
        scratch_shapes=[pltpu.VMEM((nq * HEADS_PER_STEP, TQ, LANES), F32),
                        pltpu.VMEM((HEADS_PER_STEP, TQ, seq), F32), head_tile, head_tile, head_tile],
        compiler_params=pltpu.CompilerParams(dimension_semantics=("parallel", "parallel"),
                                             vmem_limit_bytes=VMEM_LIMIT),
        name="attention",
    )(qkv, qkv, qkv, qkv, qkv, qkv, tri_u, cf_col, cf_row)


def _post_body(x_ref, ya_ref, yb_ref, g_ref, wgate_ref, bgate_ref, wua_ref, wub_ref, wout_ref, o_ref):
    x = x_ref[...]
    h = _rms(x, g_ref[...]).astype(BF16)
    ya = ya_ref[...]
    yb = yb_ref[...]
    acc = jnp.zeros(x.shape, F32)
    nchunk = D_MODEL // WIDTH
    for c in range(nchunk):
        sl_a = slice(c * WIDTH, (c + 1) * WIDTH)
        sl_b = slice(D_MODEL + c * WIDTH, D_MODEL + (c + 1) * WIDTH)
        ga = jax.nn.sigmoid(jnp.dot(h, wgate_ref[:, sl_a].astype(BF16), preferred_element_type=F32)
                            + bgate_ref[:, sl_a])
        gb = jax.nn.sigmoid(jnp.dot(h, wgate_ref[:, sl_b].astype(BF16), preferred_element_type=F32)
                            + bgate_ref[:, sl_b])
        ua = jnp.dot(ya, wua_ref[:, sl_a].astype(BF16), preferred_element_type=F32)
        ub = jnp.dot(yb, wub_ref[:, sl_a].astype(BF16), preferred_element_type=F32)
        mixed = (ga * ua + gb * ub).astype(BF16)
        acc = acc + jnp.dot(mixed, wout_ref[sl_a, :].astype(BF16), preferred_element_type=F32)
    o_ref[...] = x + acc


def _post(x2d, ya, yb, g, wgate, bgate, wua, wub, wout):
    t = x2d.shape[0]
    tile = pl.BlockSpec((TM_FFN, D_MODEL), lambda i: (i, 0))
    ytile = pl.BlockSpec((TM_FFN, WIDTH), lambda i: (i, 0))
    return pl.pallas_call(
        _post_body,
        grid=(t // TM_FFN,),
        in_specs=[tile, ytile, ytile, _const_spec((1, D_MODEL)), _const_spec((D_MODEL, 2 * D_MODEL)),
                  _const_spec((1, 2 * D_MODEL)), _const_spec((WIDTH, D_MODEL)),
                  _const_spec((WIDTH, D_MODEL)), _const_spec((D_MODEL, D_MODEL))],
        out_specs=tile,
        out_shape=jax.ShapeDtypeStruct((t, D_MODEL), F32),
        compiler_params=pltpu.CompilerParams(dimension_semantics=("parallel",),
                                             vmem_limit_bytes=VMEM_LIMIT),
        name="post",
    )(x2d, ya, yb, g, wgate, bgate, wua, wub, wout)


def kernel(x, norm_ffn1, w_ffn1_gate, w_ffn1_up, w_ffn1_down, norm_mix, w_in, b_forget, w_gate, b_gate,
           w_up_a, w_up_b, w_out, norm_ffn2, w_ffn2_gate, w_ffn2_up, w_ffn2_down, norm_final):
    batch, seq, _ = x.shape
    depth = norm_ffn1.shape[0]
    x2d = x.reshape(batch * seq, D_MODEL)
    gf = norm_final.reshape(1, D_MODEL)

    row = lax.broadcasted_iota(jnp.int32, (TM_PROJ, TM_PROJ), 0)
    col = lax.broadcasted_iota(jnp.int32, (TM_PROJ, TM_PROJ), 1)
    tri_l = (col <= row).astype(BF16)
    rk = lax.broadcasted_iota(jnp.int32, (TK, TK), 0)
    ck = lax.broadcasted_iota(jnp.int32, (TK, TK), 1)
    tri_u = (rk >= ck).astype(BF16)

    for l in range(depth):
        last = l == depth - 1
        x2d = _ffn(x2d, norm_ffn1[l].reshape(1, D_MODEL), w_ffn1_gate[l], w_ffn1_up[l], w_ffn1_down[l], gf,
                   final_norm=False)

        n_f = w_in.shape[2] - 6 * WIDTH
        w_in_t = jnp.swapaxes(w_in, 1, 2)
        w_f = jnp.pad(w_in_t[l, 6 * WIDTH:, :], ((0, LANES - n_f), (0, 0))).astype(BF16)
        bf_p = jnp.pad(b_forget[l], (0, LANES - n_f)).reshape(1, LANES)
        g_mix = norm_mix[l].reshape(1, D_MODEL)
        qkv, cf_col, cf_row = _proj(x2d, g_mix, w_in_t, l, w_f, bf_p, tri_l, batch, seq)
        y_a, y_b = _attention(qkv, tri_u, cf_col, cf_row, batch, seq)
        x2d = _post(x2d, y_a, y_b, g_mix, w_gate[l], b_gate[l].reshape(1, 2 * D_MODEL), w_up_a[l], w_up_b[l], w_out[l])

        x2d = _ffn(x2d, norm_ffn2[l].reshape(1, D_MODEL), w_ffn2_gate[l], w_ffn2_up[l], w_ffn2_down[l], gf,
                   final_norm=last)
    return x2d.reshape(batch, seq, D_MODEL)
```

```python
import functools

import jax
import jax.numpy as jnp
from jax import lax
from jax.experimental import pallas as pl
from jax.experimental.pallas import tpu as pltpu

D_MODEL = 1024
D_FF = 2816
HEAD_DIM = 64
N_HEADS = 8
WIDTH = N_HEADS * HEAD_DIM
RMS_EPS = 1e-6
ATTN_SCALE = HEAD_DIM ** -0.5
LOG2E = 1.4426950408889634

LANES = 128
HEADS_PER_BLOCK = LANES // HEAD_DIM
N_HEAD_BLOCKS = WIDTH // LANES
PAIRS_PER_STEP = 2
HEADS_PER_STEP = PAIRS_PER_STEP * HEADS_PER_BLOCK

TM_FFN = 1024
TF = 256
TM_PROJ = 512
HEAD_ROWS = 16
TK = 256
TQ = 2 * TK
VMEM_LIMIT = 60 * 1024 * 1024

F32 = jnp.float32
BF16 = jnp.bfloat16
NEG_BIG = -1e30
_CONTRACT_LAST = (((1,), (1,)), ((), ()))
SB_MASS_CUTOFF = 160.0


def _rms(x, g):
    ms = jnp.mean(x * x, axis=-1, keepdims=True)
    return x * lax.rsqrt(ms + RMS_EPS) * g


def _const_spec(shape):
    nd = len(shape)
    return pl.BlockSpec(shape, lambda *_: (0,) * nd, pipeline_mode=pl.Buffered(1))


def _ffn_body(x_ref, g_ref, wg_ref, wu_ref, wd_ref, gf_ref, o_ref, *, final_norm):
    x = x_ref[...]
    xn = _rms(x, g_ref[...]).astype(BF16)
    acc = jnp.zeros(x.shape, F32)
    for c in range(D_FF // TF):
        sl = slice(c * TF, (c + 1) * TF)
        g = jnp.dot(xn, wg_ref[:, sl].astype(BF16), preferred_element_type=F32)
        u = jnp.dot(xn, wu_ref[:, sl].astype(BF16), preferred_element_type=F32)
        h = (g * jax.nn.sigmoid(g) * u).astype(BF16)
        acc = acc + jnp.dot(h, wd_ref[sl, :].astype(BF16), preferred_element_type=F32)
    y = x + 0.5 * acc
    if final_norm:
        y = _rms(y, gf_ref[...])
    o_ref[...] = y


def _ffn(x2d, g, wg, wu, wd, gf, *, final_norm):
    t = x2d.shape[0]
    tile = pl.BlockSpec((TM_FFN, D_MODEL), lambda i: (i, 0))
    return pl.pallas_call(
        functools.partial(_ffn_body, final_norm=final_norm),
        grid=(t // TM_FFN,),
        in_specs=[tile, _const_spec((1, D_MODEL)), _const_spec((D_MODEL, D_FF)),
                  _const_spec((D_MODEL, D_FF)), _const_spec((D_FF, D_MODEL)),
                  _const_spec((1, D_MODEL))],
        out_specs=tile,
        out_shape=jax.ShapeDtypeStruct((t, D_MODEL), F32),
        compiler_params=pltpu.CompilerParams(dimension_semantics=("parallel",),
                                             vmem_limit_bytes=VMEM_LIMIT),
        name="ffn_final" if final_norm else "ffn",
    )(x2d, g, wg, wu, wd, gf)


def _proj_body(x_ref, g_ref, w_ref, wf_ref, bf_ref, tri_ref, qkv_ref, cfc_ref, cfr_ref, carry_ref):
    @pl.when(pl.program_id(1) == 0)
    def _():
        carry_ref[...] = jnp.zeros_like(carry_ref)

    h = _rms(x_ref[...], g_ref[...]).astype(BF16)

    def qkv_chunk(c):
        sl = slice(c * WIDTH, (c + 1) * WIDTH)
        p = lax.dot_general(h, w_ref[0, sl, :].astype(BF16), _CONTRACT_LAST, preferred_element_type=F32)
        if c in (0, 3):
            p = p * (ATTN_SCALE * LOG2E)
        qkv_ref[:, sl] = p.astype(BF16)

    fl = lax.dot_general(h, wf_ref[...], _CONTRACT_LAST, preferred_element_type=F32) + bf_ref[...]
    lf = (jnp.minimum(fl, 0.0) - jnp.log1p(jnp.exp(-jnp.abs(fl)))) * LOG2E
    lf_t = lf.T[:HEAD_ROWS, :]
    hi = lf_t.astype(BF16)
    r1 = lf_t - hi.astype(F32)
    mid = r1.astype(BF16)
    lo = (r1 - mid.astype(F32)).astype(BF16)
    for c in range(3):
        qkv_chunk(c)
    cs = jnp.dot(jnp.concatenate([hi, mid, lo], axis=0), tri_ref[...], preferred_element_type=F32)
    for c in range(3, 6):
        qkv_chunk(c)
    cf_t = cs[:HEAD_ROWS] + cs[HEAD_ROWS:2 * HEAD_ROWS] + cs[2 * HEAD_ROWS:] + carry_ref[:, 0:1]
    carry_ref[...] = jnp.broadcast_to(cf_t[:, TM_PROJ - 1:TM_PROJ], (HEAD_ROWS, LANES))
    cfr_ref[0] = cf_t[:N_HEADS, :]
    cfc_ref[...] = jnp.concatenate([cf_t, jnp.zeros((LANES - HEAD_ROWS, TM_PROJ), F32)], axis=0).T


def _proj(x2d, g, w_t, layer, wf_t, bf, tri, batch, seq):
    t = x2d.shape[0]
    ns = seq // TM_PROJ
    return pl.pallas_call(
        _proj_body,
        grid=(batch, ns),
        in_specs=[pl.BlockSpec((TM_PROJ, D_MODEL), lambda b, s: (b * ns + s, 0)),
                  _const_spec((1, D_MODEL)),
                  pl.BlockSpec((1,) + w_t.shape[1:], lambda b, s: (layer, 0, 0), pipeline_mode=pl.Buffered(1)),
                  _const_spec((LANES, D_MODEL)),
                  _const_spec((1, LANES)), _const_spec((TM_PROJ, TM_PROJ))],
        out_specs=[pl.BlockSpec((TM_PROJ, 6 * WIDTH), lambda b, s: (b * ns + s, 0)),
                   pl.BlockSpec((TM_PROJ, LANES), lambda b, s: (b * ns + s, 0)),
                   pl.BlockSpec((1, N_HEADS, TM_PROJ), lambda b, s: (b, 0, s))],
        out_shape=[jax.ShapeDtypeStruct((t, 6 * WIDTH), BF16),
                   jax.ShapeDtypeStruct((t, LANES), F32),
                   jax.ShapeDtypeStruct((batch, N_HEADS, seq), F32)],
        scratch_shapes=[pltpu.VMEM((HEAD_ROWS, LANES), F32)],
        compiler_params=pltpu.CompilerParams(dimension_semantics=("arbitrary", "arbitrary"),
                                             vmem_limit_bytes=VMEM_LIMIT),
        name="proj",
    )(x2d, g, w_t, wf_t, bf, tri)


def _head_split(q):
    lane = lax.broadcasted_iota(jnp.int32, q.shape, 1)
    zero = jnp.zeros_like(q)
    return [jnp.where(lane < HEAD_DIM, q, zero), jnp.where(lane >= HEAD_DIM, q, zero)]


def _qk(qh, kj):
    return lax.dot_general(qh, kj, _CONTRACT_LAST, preferred_element_type=F32)


def _lane_chunks(x):
    return [x[:, c * LANES:(c + 1) * LANES] for c in range(x.shape[1] // LANES)]


def _aligned(x, m):
    return x if isinstance(x, int) else pl.multiple_of(x, m)


def _attn_qblock(i, qa_ref, ka_ref, va_ref, qb_ref, kb_ref, vb_ref, u_ref, cfc_ref, cfr_ref, ya_ref, yb_ref,
                 acc_a, s_ref, m_ref, l_ref, acc_b):
    first_head = pl.program_id(1) * HEADS_PER_STEP
    heads = range(HEADS_PER_STEP)

    def pair_lanes(h):
        p = h // HEADS_PER_BLOCK
        return slice(p * LANES, (p + 1) * LANES)

    def head_queries(q_ref):
        return [qh for p in range(PAIRS_PER_STEP)
                for qh in _head_split(q_ref[diag_off:diag_off + TQ, p * LANES:(p + 1) * LANES])]
    lo_rows, hi_rows, all_rows = slice(0, TK), slice(TK, TQ), slice(0, TQ)
    t_idx = lax.broadcasted_iota(jnp.int32, (TK, TK), 0)
    s_idx = lax.broadcasted_iota(jnp.int32, (TK, TK), 1)
    strict = s_idx < t_idx
    causal = s_idx <= t_idx
    lane = lax.broadcasted_iota(jnp.int32, (TQ, LANES), 1)
    diag_off = i * TQ
    acc_rows = range(i * HEADS_PER_STEP, (i + 1) * HEADS_PER_STEP)

    def key_off(j):
        return _aligned(j * TK, TK)

    def sb_qk(tasks, qa):
        return [_qk(qa[h][rows], ka_ref[pl.ds(key_off(j), TK), pair_lanes(h)]) for j, h, rows, _ in tasks]

    def sb_cumsum(tasks, zs):
        cs = []
        for z, (_, _, _, diag) in zip(zs, tasks):
            sp = jnp.maximum(z, 0.0) + jnp.log2(1.0 + jnp.exp2(jnp.minimum(z, -z)))
            if diag:
                sp = jnp.where(strict, sp, 0.0)
            cs.append(jnp.dot(sp.astype(BF16), u_ref[...], preferred_element_type=F32))
        return cs

    def sb_pv(tasks, zs, cs):
        outs = []
        for z, c, (j, h, _, diag) in zip(zs, cs, tasks):
            w = jnp.exp2(z - c)
            if diag:
                w = jnp.where(strict, w, 0.0)
            o = jnp.dot(w.astype(BF16), va_ref[pl.ds(key_off(j), TK), pair_lanes(h)], preferred_element_type=F32)
            outs.append((o, c[:, 0:1]))
        return outs

    def sb_half_tasks(j, rows):
        return [(j, h, rows, False) for h in heads]

    def sb_add(res, rows, mass):
        new_mass = []
        for h in heads:
            o, t = res[h]
            acc_a[acc_rows[h], rows, :] += o * jnp.exp2(-mass[h])
            new_mass.append(mass[h] + t)
        return new_mass

    def min_mass(mass):
        return jnp.min(functools.reduce(jnp.minimum, mass))

    def sb_stream(rows, j_first, mass, lowest):
        def live(carry):
            return jnp.logical_and(carry[0] >= 0, carry[1] < SB_MASS_CUTOFF)

        def step(carry):
            j, mass = carry[0], carry[2:]
            tasks = sb_half_tasks(j, rows)
            zs = sb_qk(tasks, head_queries(qa_ref))
            new_mass = sb_add(sb_pv(tasks, zs, sb_cumsum(tasks, zs)), rows, mass)
            return (j - 1, min_mass(new_mass), *new_mass)

        return lambda: lax.while_loop(live, step, (jnp.int32(j_first), lowest, *mass))

    qb = head_queries(qb_ref)
    cfc = cfc_ref[diag_off:diag_off + TQ, :]
    fox_heads = [first_head + h for h in heads]
    cf_t = [jnp.broadcast_to(jnp.sum(jnp.where(lane == hd, cfc, 0.0), axis=1, keepdims=True), (TQ, LANES))
            for hd in fox_heads]

    def fox_qk(off, width, h, rows):
        return _qk(qb[h][rows], kb_ref[pl.ds(off, width), pair_lanes(h)])

    def fox_logits(z, off, width, h, rows):
        cf_s = cfr_ref[0, pl.ds(fox_heads[h], 1), pl.ds(off, width)]
        return jnp.concatenate([zc + cf_t[h][rows] for zc in _lane_chunks(z)], axis=1) - cf_s

    def chunk_max(s):
        return functools.reduce(jnp.maximum, _lane_chunks(s))

    def fox_probs(off, width, h, rows):
        s = s_ref[h, rows, pl.ds(off, width)]
        m = m_ref[h, rows, :]
        return jnp.concatenate([jnp.exp2(sc - m) for sc in _lane_chunks(s)], axis=1).astype(BF16)

    def fox_pv(p, off, width, h):
        v1 = jnp.concatenate([vb_ref[pl.ds(off, width), pair_lanes(h)], jnp.ones((width, LANES), BF16)], axis=1)
        o = jnp.dot(p, v1, preferred_element_type=F32)
        return o[:, :LANES], o[:, LANES:]

    qa = head_queries(qa_ref)
    diag_tasks = [t for h in heads for t in ((2 * i + 1, h, hi_rows, True), (2 * i, h, hi_rows, False),
                                             (2 * i, h, lo_rows, True))]
    zs = sb_qk(diag_tasks, qa)
    fz = [(fox_qk(diag_off, TQ, h, hi_rows), fox_qk(diag_off, TK, h, lo_rows)) for h in heads]
    cs = sb_cumsum(diag_tasks, zs)
    lowest_hi = min_mass([cs[3 * h][:, 0:1] + cs[3 * h + 1][:, 0:1] for h in heads])
    for h in heads:
        s_hi = fox_logits(fz[h][0], diag_off, TQ, h, hi_rows)
        s_hi = jnp.concatenate([s_hi[:, :TK], jnp.where(causal, s_hi[:, TK:], NEG_BIG)], axis=1)
        s_lo = jnp.where(causal, fox_logits(fz[h][1], diag_off, TK, h, lo_rows), NEG_BIG)
        s_ref[h, hi_rows, diag_off:diag_off + TQ] = s_hi
        s_ref[h, lo_rows, diag_off:diag_off + TK] = s_lo
        m_ref[h, hi_rows, :] = chunk_max(s_hi)
        m_ref[h, lo_rows, :] = chunk_max(s_lo)
    res = sb_pv(diag_tasks, zs, cs)
    mass_lo, mass_hi = [], []
    for h in heads:
        (o_b, r_b), (o_a, r_a), (o_l, r_l) = res[3 * h:3 * h + 3]
        acc_a[acc_rows[h], hi_rows, :] = o_b + o_a * jnp.exp2(-r_b)
        acc_a[acc_rows[h], lo_rows, :] = o_l
        mass_lo.append(r_l)
        mass_hi.append(r_a + r_b)

    for off in range(0, diag_off, TQ):
        for h in heads:
            s = fox_logits(fox_qk(off, TQ, h, all_rows), off, TQ, h, all_rows)
            s_ref[h, :, off:off + TQ] = s
            m_ref[h] = jnp.maximum(m_ref[h], chunk_max(s))

    for h in heads:
        m_ref[h] = jnp.broadcast_to(jnp.max(m_ref[h], axis=1, keepdims=True), (TQ, LANES))
    first_tasks = sb_half_tasks(2 * i - 1, lo_rows) if i > 0 else []
    zs = sb_qk(first_tasks, qa)
    probs = [(fox_probs(diag_off, TQ, h, hi_rows), fox_probs(diag_off, TK, h, lo_rows)) for h in heads]
    for h in heads:
        acc_b[h, hi_rows, :], l_ref[h, hi_rows, :] = fox_pv(probs[h][0], diag_off, TQ, h)
        acc_b[h, lo_rows, :], l_ref[h, lo_rows, :] = fox_pv(probs[h][1], diag_off, TK, h)
    if first_tasks:
        cs = sb_cumsum(first_tasks, zs)
        lowest_lo = min_mass([mass_lo[h] + cs[h][:, 0:1] for h in heads])
        mass_lo = sb_add(sb_pv(first_tasks, zs, cs), lo_rows, mass_lo)
    else:
        lowest_lo = min_mass(mass_lo)

    for off in range(0, diag_off, TQ):
        for h in heads:
            o, l = fox_pv(fox_probs(off, TQ, h, all_rows), off, TQ, h)
            acc_b[h] += o
            l_ref[h] += l

    yb = [acc_b[h] / l_ref[h] for h in heads]
    for p in range(PAIRS_PER_STEP):
        h0, h1 = p * HEADS_PER_BLOCK, p * HEADS_PER_BLOCK + 1
        yb_ref[diag_off:diag_off + TQ, p * LANES:(p + 1) * LANES] = (
            jnp.where(lane < HEAD_DIM, yb[h0], yb[h1]).astype(BF16))

    return [sb_stream(lo_rows, 2 * i - 2, mass_lo, lowest_lo), sb_stream(hi_rows, 2 * i - 1, mass_hi, lowest_hi)]


def _attn_body(*refs, nq):
    streams = [run for i in range(nq) for run in _attn_qblock(i, *refs)]
    for run in streams:
        run()
    ya_ref, acc_a = refs[9], refs[11]
    lane = lax.broadcasted_iota(jnp.int32, (TQ, LANES), 1)
    for i in range(nq):
        for p in range(PAIRS_PER_STEP):
            h0 = i * HEADS_PER_STEP + p * HEADS_PER_BLOCK
            ya_ref[i * TQ:(i + 1) * TQ, p * LANES:(p + 1) * LANES] = (
                jnp.where(lane < HEAD_DIM, acc_a[h0], acc_a[h0 + 1]).astype(BF16))


def _attention(qkv, tri_u, cf_col, cf_row, batch, seq):
    nb = N_HEAD_BLOCKS // PAIRS_PER_STEP
    step_lanes = PAIRS_PER_STEP * LANES

    def seq_spec(col):
        return pl.BlockSpec((seq, step_lanes), lambda b, g: (b, col + g))

    y_spec = seq_spec(0)
    y_shape = jax.ShapeDtypeStruct((batch * seq, WIDTH), BF16)
    head_tile = pltpu.VMEM((HEADS_PER_STEP, TQ, LANES), F32)
    nq = seq // TQ
    return pl.pallas_call(
        functools.partial(_attn_body, nq=nq),
        grid=(batch, nb),
        in_specs=[seq_spec(c * nb) for c in range(6)]
        + [_const_spec((TK, TK)),
           pl.BlockSpec((seq, LANES), lambda b, g: (b, 0)),
           pl.BlockSpec((1, N_HEADS, seq), lambda b, g: (b, 0, 0))],
        out_specs=[y_spec, y_spec],
        out_shape=[y_shape, y_shape],
        scratch_shapes=[pltpu.VMEM((nq * HEADS_PER_STEP, TQ, LANES), F32),
                        pltpu.VMEM((HEADS_PER_STEP, TQ, seq), F32), head_tile, head_tile, head_tile],
        compiler_params=pltpu.CompilerParams(dimension_semantics=("parallel", "parallel"),
                                             vmem_limit_bytes=VMEM_LIMIT),
        name="attention",
    )(qkv, qkv, qkv, qkv, qkv, qkv, tri_u, cf_col, cf_row)


def _post_body(x_ref, ya_ref, yb_ref, g_ref, wgate_ref, bgate_ref, wua_ref, wub_ref, wout_ref, o_ref):
    x = x_ref[...]
    h = _rms(x, g_ref[...]).astype(BF16)
    ya = ya_ref[...]
    yb = yb_ref[...]
    acc = jnp.zeros(x.shape, F32)
    nchunk = D_MODEL // WIDTH
    for c in range(nchunk):
        sl_a = slice(c * WIDTH, (c + 1) * WIDTH)
        sl_b = slice(D_MODEL + c * WIDTH, D_MODEL + (c + 1) * WIDTH)
        ga = jax.nn.sigmoid(jnp.dot(h, wgate_ref[:, sl_a].astype(BF16), preferred_element_type=F32)
                            + bgate_ref[:, sl_a])
        gb = jax.nn.sigmoid(jnp.dot(h, wgate_ref[:, sl_b].astype(BF16), preferred_element_type=F32)
                            + bgate_ref[:, sl_b])
        ua = jnp.dot(ya, wua_ref[:, sl_a].astype(BF16), preferred_element_type=F32)
        ub = jnp.dot(yb, wub_ref[:, sl_a].astype(BF16), preferred_element_type=F32)
        mixed = (ga * ua + gb * ub).astype(BF16)
        acc = acc + jnp.dot(mixed, wout_ref[sl_a, :].astype(BF16), preferred_element_type=F32)
    o_ref[...] = x + acc


def _post(x2d, ya, yb, g, wgate, bgate, wua, wub, wout):
    t = x2d.shape[0]
    tile = pl.BlockSpec((TM_FFN, D_MODEL), lambda i: (i, 0))
    ytile = pl.BlockSpec((TM_FFN, WIDTH), lambda i: (i, 0))
    return pl.pallas_call(
        _post_body,
        grid=(t // TM_FFN,),
        in_specs=[tile, ytile, ytile, _const_spec((1, D_MODEL)), _const_spec((D_MODEL, 2 * D_MODEL)),
                  _const_spec((1, 2 * D_MODEL)), _const_spec((WIDTH, D_MODEL)),
                  _const_spec((WIDTH, D_MODEL)), _const_spec((D_MODEL, D_MODEL))],
        out_specs=tile,
        out_shape=jax.ShapeDtypeStruct((t, D_MODEL), F32),
        compiler_params=pltpu.CompilerParams(dimension_semantics=("parallel",),
                                             vmem_limit_bytes=VMEM_LIMIT),
        name="post",
    )(x2d, ya, yb, g, wgate, bgate, wua, wub, wout)


def kernel(x, norm_ffn1, w_ffn1_gate, w_ffn1_up, w_ffn1_down, norm_mix, w_in, b_forget, w_gate, b_gate,
           w_up_a, w_up_b, w_out, norm_ffn2, w_ffn2_gate, w_ffn2_up, w_ffn2_down, norm_final):
    batch, seq, _ = x.shape
    depth = norm_ffn1.shape[0]
    x2d = x.reshape(batch * seq, D_MODEL)
    gf = norm_final.reshape(1, D_MODEL)

    row = lax.broadcasted_iota(jnp.int32, (TM_PROJ, TM_PROJ), 0)
    col = lax.broadcasted_iota(jnp.int32, (TM_PROJ, TM_PROJ), 1)
    tri_l = (row <= col).astype(BF16)
    rk = lax.broadcasted_iota(jnp.int32, (TK, TK), 0)
    ck = lax.broadcasted_iota(jnp.int32, (TK, TK), 1)
    tri_u = (rk >= ck).astype(BF16)

    for l in range(depth):
        last = l == depth - 1
        x2d = _ffn(x2d, norm_ffn1[l].reshape(1, D_MODEL), w_ffn1_gate[l], w_ffn1_up[l], w_ffn1_down[l], gf,
                   final_norm=False)

        n_f = w_in.shape[2] - 6 * WIDTH
        w_in_t = jnp.swapaxes(w_in, 1, 2)
        w_f = jnp.pad(w_in_t[l, 6 * WIDTH:, :], ((0, LANES - n_f), (0, 0))).astype(BF16)
        bf_p = jnp.pad(b_forget[l], (0, LANES - n_f)).reshape(1, LANES)
        g_mix = norm_mix[l].reshape(1, D_MODEL)
        qkv, cf_col, cf_row = _proj(x2d, g_mix, w_in_t, l, w_f, bf_p, tri_l, batch, seq)
        y_a, y_b = _attention(qkv, tri_u, cf_col, cf_row, batch, seq)
        x2d = _post(x2d, y_a, y_b, g_mix, w_gate[l], b_gate[l].reshape(1, 2 * D_MODEL), w_up_a[l], w_up_b[l], w_out[l])

        x2d = _ffn(x2d, norm_ffn2[l].reshape(1, D_MODEL), w_ffn2_gate[l], w_ffn2_up[l], w_ffn2_down[l], gf,
                   final_norm=last)
    return x2d.reshape(batch, seq, D_MODEL)
```

```python
import functools

import jax
import jax.numpy as jnp
from jax import lax
from jax.experimental import pallas as pl
from jax.experimental.pallas import tpu as pltpu

D_MODEL = 1024
D_FF = 2816
HEAD_DIM = 64
N_HEADS = 8
WIDTH = N_HEADS * HEAD_DIM
RMS_EPS = 1e-6
ATTN_SCALE = HEAD_DIM ** -0.5
LOG2E = 1.4426950408889634

LANES = 128
HEADS_PER_BLOCK = LANES // HEAD_DIM
N_HEAD_BLOCKS = WIDTH // LANES
PAIRS_PER_STEP = 2
HEADS_PER_STEP = PAIRS_PER_STEP * HEADS_PER_BLOCK

TM_FFN = 1024
TF = 256
TM_PROJ = 512
HEAD_ROWS = 16
TK = 256
TQ = 2 * TK
VMEM_LIMIT = 60 * 1024 * 1024

F32 = jnp.float32
BF16 = jnp.bfloat16
NEG_BIG = -1e30
_CONTRACT_LAST = (((1,), (1,)), ((), ()))
SB_MASS_CUTOFF = 160.0


def _rms(x, g):
    ms = jnp.mean(x * x, axis=-1, keepdims=True)
    return x * lax.rsqrt(ms + RMS_EPS) * g


def _const_spec(shape):
    nd = len(shape)
    return pl.BlockSpec(shape, lambda *_: (0,) * nd, pipeline_mode=pl.Buffered(1))


def _ffn_body(x_ref, g_ref, wg_ref, wu_ref, wd_ref, gf_ref, o_ref, *, final_norm):
    x = x_ref[...]
    xn = _rms(x, g_ref[...]).astype(BF16)
    acc = jnp.zeros(x.shape, F32)
    for c in range(D_FF // TF):
        sl = slice(c * TF, (c + 1) * TF)
        g = jnp.dot(xn, wg_ref[:, sl].astype(BF16), preferred_element_type=F32)
        u = jnp.dot(xn, wu_ref[:, sl].astype(BF16), preferred_element_type=F32)
        h = (g * jax.nn.sigmoid(g) * u).astype(BF16)
        acc = acc + jnp.dot(h, wd_ref[sl, :].astype(BF16), preferred_element_type=F32)
    y = x + 0.5 * acc
    if final_norm:
        y = _rms(y, gf_ref[...])
    o_ref[...] = y


def _ffn(x2d, g, wg, wu, wd, gf, *, final_norm):
    t = x2d.shape[0]
    tile = pl.BlockSpec((TM_FFN, D_MODEL), lambda i: (i, 0))
    return pl.pallas_call(
        functools.partial(_ffn_body, final_norm=final_norm),
        grid=(t // TM_FFN,),
        in_specs=[tile, _const_spec((1, D_MODEL)), _const_spec((D_MODEL, D_FF)),
                  _const_spec((D_MODEL, D_FF)), _const_spec((D_FF, D_MODEL)),
                  _const_spec((1, D_MODEL))],
        out_specs=tile,
        out_shape=jax.ShapeDtypeStruct((t, D_MODEL), F32),
        compiler_params=pltpu.CompilerParams(dimension_semantics=("parallel",),
                                             vmem_limit_bytes=VMEM_LIMIT),
        name="ffn_final" if final_norm else "ffn",
    )(x2d, g, wg, wu, wd, gf)


def _proj_body(x_ref, g_ref, w_ref, wf_ref, bf_ref, tri_ref, qkv_ref, cfc_ref, cfr_ref, carry_ref):
    @pl.when(pl.program_id(1) == 0)
    def _():
        carry_ref[...] = jnp.zeros_like(carry_ref)

    h = _rms(x_ref[...], g_ref[...]).astype(BF16)

    def qkv_chunk(c):
        sl = slice(c * WIDTH, (c + 1) * WIDTH)
        p = lax.dot_general(h, w_ref[0, sl, :].astype(BF16), _CONTRACT_LAST, preferred_element_type=F32)
        if c in (0, 3):
            p = p * (ATTN_SCALE * LOG2E)
        qkv_ref[:, sl] = p.astype(BF16)

    fl = lax.dot_general(h, wf_ref[...], _CONTRACT_LAST, preferred_element_type=F32) + bf_ref[...]
    lf = (jnp.minimum(fl, 0.0) - jnp.log1p(jnp.exp(-jnp.abs(fl)))) * LOG2E
    lf_t = lf.T[:HEAD_ROWS, :]
    hi = lf_t.astype(BF16)
    r1 = lf_t - hi.astype(F32)
    mid = r1.astype(BF16)
    lo = (r1 - mid.astype(F32)).astype(BF16)
    for c in range(3):
        qkv_chunk(c)
    cs = jnp.dot(jnp.concatenate([hi, mid, lo], axis=0), tri_ref[...], preferred_element_type=F32)
    for c in range(3, 6):
        qkv_chunk(c)
    cf_t = cs[:HEAD_ROWS] + cs[HEAD_ROWS:2 * HEAD_ROWS] + cs[2 * HEAD_ROWS:] + carry_ref[:, 0:1]
    carry_ref[...] = jnp.broadcast_to(cf_t[:, TM_PROJ - 1:TM_PROJ], (HEAD_ROWS, LANES))
    cfr_ref[0] = cf_t[:N_HEADS, :]
    cfc_ref[...] = jnp.concatenate([cf_t, jnp.zeros((LANES - HEAD_ROWS, TM_PROJ), F32)], axis=0).T


def _proj(x2d, g, w_t, layer, wf_t, bf, tri, batch, seq):
    t = x2d.shape[0]
    ns = seq // TM_PROJ
    return pl.pallas_call(
        _proj_body,
        grid=(batch, ns),
        in_specs=[pl.BlockSpec((TM_PROJ, D_MODEL), lambda b, s: (b * ns + s, 0)),
                  _const_spec((1, D_MODEL)),
                  pl.BlockSpec((1,) + w_t.shape[1:], lambda b, s: (layer, 0, 0), pipeline_mode=pl.Buffered(1)),
                  _const_spec((LANES, D_MODEL)),
                  _const_spec((1, LANES)), _const_spec((TM_PROJ, TM_PROJ))],
        out_specs=[pl.BlockSpec((TM_PROJ, 6 * WIDTH), lambda b, s: (b * ns + s, 0)),
                   pl.BlockSpec((TM_PROJ, LANES), lambda b, s: (b * ns + s, 0)),
                   pl.BlockSpec((1, N_HEADS, TM_PROJ), lambda b, s: (b, 0, s))],
        out_shape=[jax.ShapeDtypeStruct((t, 6 * WIDTH), BF16),
                   jax.ShapeDtypeStruct((t, LANES), F32),
                   jax.ShapeDtypeStruct((batch, N_HEADS, seq), F32)],
        scratch_shapes=[pltpu.VMEM((HEAD_ROWS, LANES), F32)],
        compiler_params=pltpu.CompilerParams(dimension_semantics=("arbitrary", "arbitrary"),
                                             vmem_limit_bytes=VMEM_LIMIT),
        name="proj",
    )(x2d, g, w_t, wf_t, bf, tri)


def _head_split(q):
    lane = lax.broadcasted_iota(jnp.int32, q.shape, 1)
    zero = jnp.zeros_like(q)
    return [jnp.where(lane < HEAD_DIM, q, zero), jnp.where(lane >= HEAD_DIM, q, zero)]


def _qk(qh, kj):
    return lax.dot_general(qh, kj, _CONTRACT_LAST, preferred_element_type=F32)


def _lane_chunks(x):
    return [x[:, c * LANES:(c + 1) * LANES] for c in range(x.shape[1] // LANES)]


def _aligned(x, m):
    return x if isinstance(x, int) else pl.multiple_of(x, m)


def _attn_qblock(i, qa_ref, ka_ref, va_ref, qb_ref, kb_ref, vb_ref, u_ref, cfc_ref, cfr_ref, ya_ref, yb_ref,
                 acc_a, s_ref, m_ref, l_ref, acc_b):
    first_head = pl.program_id(1) * HEADS_PER_STEP
    heads = range(HEADS_PER_STEP)

    def pair_lanes(h):
        p = h // HEADS_PER_BLOCK
        return slice(p * LANES, (p + 1) * LANES)

    def head_queries(q_ref):
        return [qh for p in range(PAIRS_PER_STEP)
                for qh in _head_split(q_ref[diag_off:diag_off + TQ, p * LANES:(p + 1) * LANES])]
    lo_rows, hi_rows, all_rows = slice(0, TK), slice(TK, TQ), slice(0, TQ)
    t_idx = lax.broadcasted_iota(jnp.int32, (TK, TK), 0)
    s_idx = lax.broadcasted_iota(jnp.int32, (TK, TK), 1)
    strict = s_idx < t_idx
    causal = s_idx <= t_idx
    lane = lax.broadcasted_iota(jnp.int32, (TQ, LANES), 1)
    diag_off = i * TQ
    acc_rows = range(i * HEADS_PER_STEP, (i + 1) * HEADS_PER_STEP)

    def key_off(j):
        return _aligned(j * TK, TK)

    def sb_qk(tasks, qa):
        return [_qk(qa[h][rows], ka_ref[pl.ds(key_off(j), TK), pair_lanes(h)]) for j, h, rows, _ in tasks]

    def sb_cumsum(tasks, zs):
        outs = []
        for z, (_, _, _, diag) in zip(zs, tasks):
            sp = jnp.maximum(z, 0.0) + jnp.log2(1.0 + jnp.exp2(jnp.minimum(z, -z)))
            log_beta = z - sp
            if diag:
                sp = jnp.where(strict, sp, 0.0)
            later = jnp.dot(sp.astype(BF16), u_ref[...], preferred_element_type=F32)
            outs.append((log_beta - later, later[:, 0:1] + sp[:, 0:1]))
        return outs

    def sb_pv(tasks, zs, cs):
        outs = []
        for (log_w, tot), (j, h, _, diag) in zip(cs, tasks):
            w = jnp.exp2(log_w)
            if diag:
                w = jnp.where(strict, w, 0.0)
            o = jnp.dot(w.astype(BF16), va_ref[pl.ds(key_off(j), TK), pair_lanes(h)], preferred_element_type=F32)
            outs.append((o, tot))
        return outs

    def sb_half_tasks(j, rows):
        return [(j, h, rows, False) for h in heads]

    def sb_add(res, rows, mass):
        new_mass = []
        for h in heads:
            o, t = res[h]
            acc_a[acc_rows[h], rows, :] += o * jnp.exp2(-mass[h])
            new_mass.append(mass[h] + t)
        return new_mass

    def min_mass(mass):
        return jnp.min(functools.reduce(jnp.minimum, mass))

    def sb_stream(rows, j_first, mass, lowest):
        def live(carry):
            return jnp.logical_and(carry[0] >= 0, carry[1] < SB_MASS_CUTOFF)

        def step(carry):
            j, mass = carry[0], carry[2:]
            tasks = sb_half_tasks(j, rows)
            zs = sb_qk(tasks, head_queries(qa_ref))
            new_mass = sb_add(sb_pv(tasks, zs, sb_cumsum(tasks, zs)), rows, mass)
            return (j - 1, min_mass(new_mass), *new_mass)

        return lambda: lax.while_loop(live, step, (jnp.int32(j_first), lowest, *mass))

    qb = head_queries(qb_ref)
    cfc = cfc_ref[diag_off:diag_off + TQ, :]
    fox_heads = [first_head + h for h in heads]
    cf_t = [jnp.broadcast_to(jnp.sum(jnp.where(lane == hd, cfc, 0.0), axis=1, keepdims=True), (TQ, LANES))
            for hd in fox_heads]

    def fox_qk(off, width, h, rows):
        return _qk(qb[h][rows], kb_ref[pl.ds(off, width), pair_lanes(h)])

    def fox_logits(z, off, width, h, rows):
        cf_s = cfr_ref[0, pl.ds(fox_heads[h], 1), pl.ds(off, width)]
        return jnp.concatenate([zc + cf_t[h][rows] for zc in _lane_chunks(z)], axis=1) - cf_s

    def chunk_max(s):
        return functools.reduce(jnp.maximum, _lane_chunks(s))

    def fox_probs(off, width, h, rows):
        s = s_ref[h, rows, pl.ds(off, width)]
        m = m_ref[h, rows, :]
        return jnp.concatenate([jnp.exp2(sc - m) for sc in _lane_chunks(s)], axis=1).astype(BF16)

    def fox_pv(p, off, width, h):
        v1 = jnp.concatenate([vb_ref[pl.ds(off, width), pair_lanes(h)], jnp.ones((width, LANES), BF16)], axis=1)
        o = jnp.dot(p, v1, preferred_element_type=F32)
        return o[:, :LANES], o[:, LANES:]

    qa = head_queries(qa_ref)
    diag_tasks = [t for h in heads for t in ((2 * i + 1, h, hi_rows, True), (2 * i, h, hi_rows, False),
                                             (2 * i, h, lo_rows, True))]
    zs = sb_qk(diag_tasks, qa)
    fz = [(fox_qk(diag_off, TQ, h, hi_rows), fox_qk(diag_off, TK, h, lo_rows)) for h in heads]
    cs = sb_cumsum(diag_tasks, zs)
    lowest_hi = min_mass([cs[3 * h][1] + cs[3 * h + 1][1] for h in heads])
    for h in heads:
        s_hi = fox_logits(fz[h][0], diag_off, TQ, h, hi_rows)
        s_hi = jnp.concatenate([s_hi[:, :TK], jnp.where(causal, s_hi[:, TK:], NEG_BIG)], axis=1)
        s_lo = jnp.where(causal, fox_logits(fz[h][1], diag_off, TK, h, lo_rows), NEG_BIG)
        s_ref[h, hi_rows, diag_off:diag_off + TQ] = s_hi
        s_ref[h, lo_rows, diag_off:diag_off + TK] = s_lo
        m_ref[h, hi_rows, :] = chunk_max(s_hi)
        m_ref[h, lo_rows, :] = chunk_max(s_lo)
    res = sb_pv(diag_tasks, zs, cs)
    mass_lo, mass_hi = [], []
    for h in heads:
        (o_b, r_b), (o_a, r_a), (o_l, r_l) = res[3 * h:3 * h + 3]
        acc_a[acc_rows[h], hi_rows, :] = o_b + o_a * jnp.exp2(-r_b)
        acc_a[acc_rows[h], lo_rows, :] = o_l
        mass_lo.append(r_l)
        mass_hi.append(r_a + r_b)

    for off in range(0, diag_off, TQ):
        for h in heads:
            s = fox_logits(fox_qk(off, TQ, h, all_rows), off, TQ, h, all_rows)
            s_ref[h, :, off:off + TQ] = s
            m_ref[h] = jnp.maximum(m_ref[h], chunk_max(s))

    for h in heads:
        m_ref[h] = jnp.broadcast_to(jnp.max(m_ref[h], axis=1, keepdims=True), (TQ, LANES))
    first_tasks = sb_half_tasks(2 * i - 1, lo_rows) if i > 0 else []
    zs = sb_qk(first_tasks, qa)
    probs = [(fox_probs(diag_off, TQ, h, hi_rows), fox_probs(diag_off, TK, h, lo_rows)) for h in heads]
    for h in heads:
        acc_b[h, hi_rows, :], l_ref[h, hi_rows, :] = fox_pv(probs[h][0], diag_off, TQ, h)
        acc_b[h, lo_rows, :], l_ref[h, lo_rows, :] = fox_pv(probs[h][1], diag_off, TK, h)
    if first_tasks:
        cs = sb_cumsum(first_tasks, zs)
        lowest_lo = min_mass([mass_lo[h] + cs[h][1] for h in heads])
        mass_lo = sb_add(sb_pv(first_tasks, zs, cs), lo_rows, mass_lo)
    else:
        lowest_lo = min_mass(mass_lo)

    for off in range(0, diag_off, TQ):
        for h in heads:
            o, l = fox_pv(fox_probs(off, TQ, h, all_rows), off, TQ, h)
            acc_b[h] += o
            l_ref[h] += l

    yb = [acc_b[h] / l_ref[h] for h in heads]
    for p in range(PAIRS_PER_STEP):
        h0, h1 = p * HEADS_PER_BLOCK, p * HEADS_PER_BLOCK + 1
        yb_ref[diag_off:diag_off + TQ, p * LANES:(p + 1) * LANES] = (
            jnp.where(lane < HEAD_DIM, yb[h0], yb[h1]).astype(BF16))

    return [sb_stream(lo_rows, 2 * i - 2, mass_lo, lowest_lo), sb_stream(hi_rows, 2 * i - 1, mass_hi, lowest_hi)]


def _attn_body(*refs, nq):
    streams = [run for i in range(nq) for run in _attn_qblock(i, *refs)]
    for run in streams:
        run()
    ya_ref, acc_a = refs[9], refs[11]
    lane = lax.broadcasted_iota(jnp.int32, (TQ, LANES), 1)
    for i in range(nq):
        for p in range(PAIRS_PER_STEP):
            h0 = i * HEADS_PER_STEP + p * HEADS_PER_BLOCK
            ya_ref[i * TQ:(i + 1) * TQ, p * LANES:(p + 1) * LANES] = (
                jnp.where(lane < HEAD_DIM, acc_a[h0], acc_a[h0 + 1]).astype(BF16))


def _attention(qkv, tri_u, cf_col, cf_row, batch, seq):
    nb = N_HEAD_BLOCKS // PAIRS_PER_STEP
    step_lanes = PAIRS_PER_STEP * LANES

    def seq_spec(col):
        return pl.BlockSpec((seq, step_lanes), lambda b, g: (b, col + g))

    y_spec = seq_spec(0)
    y_shape = jax.ShapeDtypeStruct((batch * seq, WIDTH), BF16)
    head_tile = pltpu.VMEM((HEADS_PER_STEP, TQ, LANES), F32)
    nq = seq // TQ
    return pl.pallas_call(
        functools.partial(_attn_body, nq=nq),
        grid=(batch, nb),
        in_specs=[seq_spec(c * nb) for c in range(6)]
        + [_const_spec((TK, TK)),
           pl.BlockSpec((seq, LANES), lambda b, g: (b, 0)),
           pl.BlockSpec((1, N_HEADS, seq), lambda b, g: (b, 0, 0))],
        out_specs=[y_spec, y_spec],
        out_shape=[y_shape, y_shape],
        scratch_shapes=[pltpu.VMEM((nq * HEADS_PER_STEP, TQ, LANES), F32),
                        pltpu.VMEM((HEADS_PER_STEP, TQ, seq), F32), head_tile, head_tile, head_tile],
        compiler_params=pltpu.CompilerParams(dimension_semantics=("parallel", "parallel"),
                                             vmem_limit_bytes=VMEM_LIMIT),
        name="attention",
    )(qkv, qkv, qkv, qkv, qkv, qkv, tri_u, cf_col, cf_row)


def _post_body(x_ref, ya_ref, yb_ref, g_ref, wgate_ref, bgate_ref, wua_ref, wub_ref, wout_ref, o_ref):
    x = x_ref[...]
    h = _rms(x, g_ref[...]).astype(BF16)
    ya = ya_ref[...]
    yb = yb_ref[...]
    acc = jnp.zeros(x.shape, F32)
    nchunk = D_MODEL // WIDTH
    for c in range(nchunk):
        sl_a = slice(c * WIDTH, (c + 1) * WIDTH)
        sl_b = slice(D_MODEL + c * WIDTH, D_MODEL + (c + 1) * WIDTH)
        ga = jax.nn.sigmoid(jnp.dot(h, wgate_ref[:, sl_a].astype(BF16), preferred_element_type=F32)
                            + bgate_ref[:, sl_a])
        gb = jax.nn.sigmoid(jnp.dot(h, wgate_ref[:, sl_b].astype(BF16), preferred_element_type=F32)
                            + bgate_ref[:, sl_b])
        ua = jnp.dot(ya, wua_ref[:, sl_a].astype(BF16), preferred_element_type=F32)
        ub = jnp.dot(yb, wub_ref[:, sl_a].astype(BF16), preferred_element_type=F32)
        mixed = (ga * ua + gb * ub).astype(BF16)
        acc = acc + jnp.dot(mixed, wout_ref[sl_a, :].astype(BF16), preferred_element_type=F32)
    o_ref[...] = x + acc


def _post(x2d, ya, yb, g, wgate, bgate, wua, wub, wout):
    t = x2d.shape[0]
    tile = pl.BlockSpec((TM_FFN, D_MODEL), lambda i: (i, 0))
    ytile = pl.BlockSpec((TM_FFN, WIDTH), lambda i: (i, 0))
    return pl.pallas_call(
        _post_body,
        grid=(t // TM_FFN,),
        in_specs=[tile, ytile, ytile, _const_spec((1, D_MODEL)), _const_spec((D_MODEL, 2 * D_MODEL)),
                  _const_spec((1, 2 * D_MODEL)), _const_spec((WIDTH, D_MODEL)),
                  _const_spec((WIDTH, D_MODEL)), _const_spec((D_MODEL, D_MODEL))],
        out_specs=tile,
        out_shape=jax.ShapeDtypeStruct((t, D_MODEL), F32),
        compiler_params=pltpu.CompilerParams(dimension_semantics=("parallel",),
                                             vmem_limit_bytes=VMEM_LIMIT),
        name="post",
    )(x2d, ya, yb, g, wgate, bgate, wua, wub, wout)


def kernel(x, norm_ffn1, w_ffn1_gate, w_ffn1_up, w_ffn1_down, norm_mix, w_in, b_forget, w_gate, b_gate,
           w_up_a, w_up_b, w_out, norm_ffn2, w_ffn2_gate, w_ffn2_up, w_ffn2_down, norm_final):
    batch, seq, _ = x.shape
    depth = norm_ffn1.shape[0]
    x2d = x.reshape(batch * seq, D_MODEL)
    gf = norm_final.reshape(1, D_MODEL)

    row = lax.broadcasted_iota(jnp.int32, (TM_PROJ, TM_PROJ), 0)
    col = lax.broadcasted_iota(jnp.int32, (TM_PROJ, TM_PROJ), 1)
    tri_l = (row <= col).astype(BF16)
    rk = lax.broadcasted_iota(jnp.int32, (TK, TK), 0)
    ck = lax.broadcasted_iota(jnp.int32, (TK, TK), 1)
    tri_u = (rk > ck).astype(BF16)

    for l in range(depth):
        last = l == depth - 1
        x2d = _ffn(x2d, norm_ffn1[l].reshape(1, D_MODEL), w_ffn1_gate[l], w_ffn1_up[l], w_ffn1_down[l], gf,
                   final_norm=False)

        n_f = w_in.shape[2] - 6 * WIDTH
        w_in_t = jnp.swapaxes(w_in, 1, 2)
        w_f = jnp.pad(w_in_t[l, 6 * WIDTH:, :], ((0, LANES - n_f), (0, 0))).astype(BF16)
        bf_p = jnp.pad(b_forget[l], (0, LANES - n_f)).reshape(1, LANES)
        g_mix = norm_mix[l].reshape(1, D_MODEL)
        qkv, cf_col, cf_row = _proj(x2d, g_mix, w_in_t, l, w_f, bf_p, tri_l, batch, seq)
        y_a, y_b = _attention(qkv, tri_u, cf_col, cf_row, batch, seq)
        x2d = _post(x2d, y_a, y_b, g_mix, w_gate[l], b_gate[l].reshape(1, 2 * D_MODEL), w_up_a[l], w_up_b[l], w_out[l])

        x2d = _ffn(x2d, norm_ffn2[l].reshape(1, D_MODEL), w_ffn2_gate[l], w_ffn2_up[l], w_ffn2_down[l], gf,
                   final_norm=last)
    return x2d.reshape(batch, seq, D_MODEL)
```

```python
import functools

import jax
import jax.numpy as jnp
from jax import lax
from jax.experimental import pallas as pl
from jax.experimental.pallas import tpu as pltpu

D_MODEL = 1024
D_FF = 2816
HEAD_DIM = 64
N_HEADS = 8
WIDTH = N_HEADS * HEAD_DIM
RMS_EPS = 1e-6
ATTN_SCALE = HEAD_DIM ** -0.5
LOG2E = 1.4426950408889634

LANES = 128
HEADS_PER_BLOCK = LANES // HEAD_DIM
N_HEAD_BLOCKS = WIDTH // LANES
PAIRS_PER_STEP = 2
HEADS_PER_STEP = PAIRS_PER_STEP * HEADS_PER_BLOCK

TM_FFN = 1024
TF = 256
TM_PROJ = 512
HEAD_ROWS = 16
TK = 256
TQ = 2 * TK
VMEM_LIMIT = 60 * 1024 * 1024

F32 = jnp.float32
BF16 = jnp.bfloat16
NEG_BIG = -1e30
_CONTRACT_LAST = (((1,), (1,)), ((), ()))
SB_MASS_CUTOFF = 160.0
SB_PV_LAG = 4


def _rms(x, g):
    ms = jnp.mean(x * x, axis=-1, keepdims=True)
    return x * lax.rsqrt(ms + RMS_EPS) * g


def _const_spec(shape):
    nd = len(shape)
    return pl.BlockSpec(shape, lambda *_: (0,) * nd, pipeline_mode=pl.Buffered(1))


def _ffn_body(x_ref, g_ref, wg_ref, wu_ref, wd_ref, gf_ref, o_ref, *, final_norm):
    x = x_ref[...]
    xn = _rms(x, g_ref[...]).astype(BF16)
    acc = jnp.zeros(x.shape, F32)
    for c in range(D_FF // TF):
        sl = slice(c * TF, (c + 1) * TF)
        g = jnp.dot(xn, wg_ref[:, sl].astype(BF16), preferred_element_type=F32)
        u = jnp.dot(xn, wu_ref[:, sl].astype(BF16), preferred_element_type=F32)
        h = (g * jax.nn.sigmoid(g) * u).astype(BF16)
        acc = acc + jnp.dot(h, wd_ref[sl, :].astype(BF16), preferred_element_type=F32)
    y = x + 0.5 * acc
    if final_norm:
        y = _rms(y, gf_ref[...])
    o_ref[...] = y


def _ffn(x2d, g, wg, wu, wd, gf, *, final_norm):
    t = x2d.shape[0]
    tile = pl.BlockSpec((TM_FFN, D_MODEL), lambda i: (i, 0))
    return pl.pallas_call(
        functools.partial(_ffn_body, final_norm=final_norm),
        grid=(t // TM_FFN,),
        in_specs=[tile, _const_spec((1, D_MODEL)), _const_spec((D_MODEL, D_FF)),
                  _const_spec((D_MODEL, D_FF)), _const_spec((D_FF, D_MODEL)),
                  _const_spec((1, D_MODEL))],
        out_specs=tile,
        out_shape=jax.ShapeDtypeStruct((t, D_MODEL), F32),
        compiler_params=pltpu.CompilerParams(dimension_semantics=("parallel",),
                                             vmem_limit_bytes=VMEM_LIMIT),
        name="ffn_final" if final_norm else "ffn",
    )(x2d, g, wg, wu, wd, gf)


def _proj_body(x_ref, g_ref, w_ref, wf_ref, bf_ref, tri_ref, qkv_ref, cfc_ref, cfr_ref, carry_ref):
    @pl.when(pl.program_id(1) == 0)
    def _():
        carry_ref[...] = jnp.zeros_like(carry_ref)

    h = _rms(x_ref[...], g_ref[...]).astype(BF16)

    def qkv_chunk(c):
        sl = slice(c * WIDTH, (c + 1) * WIDTH)
        p = lax.dot_general(h, w_ref[0, sl, :].astype(BF16), _CONTRACT_LAST, preferred_element_type=F32)
        if c in (0, 3):
            p = p * (ATTN_SCALE * LOG2E)
        qkv_ref[:, sl] = p.astype(BF16)

    fl = lax.dot_general(h, wf_ref[...], _CONTRACT_LAST, preferred_element_type=F32) + bf_ref[...]
    lf = (jnp.minimum(fl, 0.0) - jnp.log1p(jnp.exp(-jnp.abs(fl)))) * LOG2E
    lf_t = lf.T[:HEAD_ROWS, :]
    hi = lf_t.astype(BF16)
    r1 = lf_t - hi.astype(F32)
    mid = r1.astype(BF16)
    lo = (r1 - mid.astype(F32)).astype(BF16)
    for c in range(3):
        qkv_chunk(c)
    cs = jnp.dot(jnp.concatenate([hi, mid, lo], axis=0), tri_ref[...], preferred_element_type=F32)
    for c in range(3, 6):
        qkv_chunk(c)
    cf_t = cs[:HEAD_ROWS] + cs[HEAD_ROWS:2 * HEAD_ROWS] + cs[2 * HEAD_ROWS:] + carry_ref[:, 0:1]
    carry_ref[...] = jnp.broadcast_to(cf_t[:, TM_PROJ - 1:TM_PROJ], (HEAD_ROWS, LANES))
    cfr_ref[0] = cf_t[:N_HEADS, :]
    cfc_ref[...] = jnp.concatenate([cf_t, jnp.zeros((LANES - HEAD_ROWS, TM_PROJ), F32)], axis=0).T


def _proj(x2d, g, w_t, layer, wf_t, bf, tri, batch, seq):
    t = x2d.shape[0]
    ns = seq // TM_PROJ
    return pl.pallas_call(
        _proj_body,
        grid=(batch, ns),
        in_specs=[pl.BlockSpec((TM_PROJ, D_MODEL), lambda b, s: (b * ns + s, 0)),
                  _const_spec((1, D_MODEL)),
                  pl.BlockSpec((1,) + w_t.shape[1:], lambda b, s: (layer, 0, 0), pipeline_mode=pl.Buffered(1)),
                  _const_spec((LANES, D_MODEL)),
                  _const_spec((1, LANES)), _const_spec((TM_PROJ, TM_PROJ))],
        out_specs=[pl.BlockSpec((TM_PROJ, 6 * WIDTH), lambda b, s: (b * ns + s, 0)),
                   pl.BlockSpec((TM_PROJ, LANES), lambda b, s: (b * ns + s, 0)),
                   pl.BlockSpec((1, N_HEADS, TM_PROJ), lambda b, s: (b, 0, s))],
        out_shape=[jax.ShapeDtypeStruct((t, 6 * WIDTH), BF16),
                   jax.ShapeDtypeStruct((t, LANES), F32),
                   jax.ShapeDtypeStruct((batch, N_HEADS, seq), F32)],
        scratch_shapes=[pltpu.VMEM((HEAD_ROWS, LANES), F32)],
        compiler_params=pltpu.CompilerParams(dimension_semantics=("arbitrary", "arbitrary"),
                                             vmem_limit_bytes=VMEM_LIMIT),
        name="proj",
    )(x2d, g, w_t, wf_t, bf, tri)


def _head_split(q):
    lane = lax.broadcasted_iota(jnp.int32, q.shape, 1)
    zero = jnp.zeros_like(q)
    return [jnp.where(lane < HEAD_DIM, q, zero), jnp.where(lane >= HEAD_DIM, q, zero)]


def _qk(qh, kj):
    return lax.dot_general(qh, kj, _CONTRACT_LAST, preferred_element_type=F32)


def _lane_chunks(x):
    return [x[:, c * LANES:(c + 1) * LANES] for c in range(x.shape[1] // LANES)]


def _aligned(x, m):
    return x if isinstance(x, int) else pl.multiple_of(x, m)


def _attn_qblock(i, qa_ref, ka_ref, va_ref, qb_ref, kb_ref, vb_ref, u_ref, cfc_ref, cfr_ref, ya_ref, yb_ref,
                 acc_a, s_ref, m_ref, l_ref, acc_b):
    first_head = pl.program_id(1) * HEADS_PER_STEP
    heads = range(HEADS_PER_STEP)

    def pair_lanes(h):
        p = h // HEADS_PER_BLOCK
        return slice(p * LANES, (p + 1) * LANES)

    def head_queries(q_ref):
        return [qh for p in range(PAIRS_PER_STEP)
                for qh in _head_split(q_ref[diag_off:diag_off + TQ, p * LANES:(p + 1) * LANES])]
    lo_rows, hi_rows, all_rows = slice(0, TK), slice(TK, TQ), slice(0, TQ)
    t_idx = lax.broadcasted_iota(jnp.int32, (TK, TK), 0)
    s_idx = lax.broadcasted_iota(jnp.int32, (TK, TK), 1)
    strict = s_idx < t_idx
    causal = s_idx <= t_idx
    lane = lax.broadcasted_iota(jnp.int32, (TQ, LANES), 1)
    diag_off = i * TQ
    acc_rows = range(i * HEADS_PER_STEP, (i + 1) * HEADS_PER_STEP)

    def key_off(j):
        return _aligned(j * TK, TK)

    def sb_qk(tasks, qa):
        return [_qk(qa[h][rows], ka_ref[pl.ds(key_off(j), TK), pair_lanes(h)]) for j, h, rows, _ in tasks]

    def sb_cumsum(tasks, zs):
        outs = []
        for z, (_, _, _, diag) in zip(zs, tasks):
            sp = jnp.maximum(z, 0.0) + jnp.log2(1.0 + jnp.exp2(jnp.minimum(z, -z)))
            log_beta = z - sp
            if diag:
                sp = jnp.where(strict, sp, 0.0)
            later = jnp.dot(sp.astype(BF16), u_ref[...], preferred_element_type=F32)
            outs.append((log_beta - later, later[:, 0:1] + sp[:, 0:1]))
        return outs

    def sb_pv(tasks, zs, cs):
        outs = []
        for (log_w, tot), (j, h, _, diag) in zip(cs, tasks):
            w = jnp.exp2(log_w)
            if diag:
                w = jnp.where(strict, w, 0.0)
            o = jnp.dot(w.astype(BF16), va_ref[pl.ds(key_off(j), TK), pair_lanes(h)], preferred_element_type=F32)
            outs.append((o, tot))
        return outs

    def sb_half_tasks(j, rows):
        return [(j, h, rows, False) for h in heads]

    def sb_add(res, rows, mass):
        new_mass = []
        for h in heads:
            o, t = res[h]
            acc_a[acc_rows[h], rows, :] += o * jnp.exp2(-mass[h])
            new_mass.append(mass[h] + t)
        return new_mass

    def min_mass(mass):
        return jnp.min(functools.reduce(jnp.minimum, mass))

    def sb_stream(rows, j_first, mass, lowest):
        def live(carry):
            return jnp.logical_and(carry[0] >= 0, carry[1] < SB_MASS_CUTOFF)

        def step(carry):
            j, mass = carry[0], carry[2:]
            tasks = sb_half_tasks(j, rows)
            zs = sb_qk(tasks, head_queries(qa_ref))
            new_mass = sb_add(sb_pv(tasks, zs, sb_cumsum(tasks, zs)), rows, mass)
            return (j - 1, min_mass(new_mass), *new_mass)

        return lambda: lax.while_loop(live, step, (jnp.int32(j_first), lowest, *mass))

    qb = head_queries(qb_ref)
    cfc = cfc_ref[diag_off:diag_off + TQ, :]
    fox_heads = [first_head + h for h in heads]
    cf_t = [jnp.broadcast_to(jnp.sum(jnp.where(lane == hd, cfc, 0.0), axis=1, keepdims=True), (TQ, LANES))
            for hd in fox_heads]

    def fox_qk(off, width, h, rows):
        return _qk(qb[h][rows], kb_ref[pl.ds(off, width), pair_lanes(h)])

    def fox_logits(z, off, width, h, rows):
        cf_s = cfr_ref[0, pl.ds(fox_heads[h], 1), pl.ds(off, width)]
        return jnp.concatenate([zc + cf_t[h][rows] for zc in _lane_chunks(z)], axis=1) - cf_s

    def chunk_max(s):
        return functools.reduce(jnp.maximum, _lane_chunks(s))

    def fox_probs(off, width, h, rows):
        s = s_ref[h, rows, pl.ds(off, width)]
        m = m_ref[h, rows, :]
        return jnp.concatenate([jnp.exp2(sc - m) for sc in _lane_chunks(s)], axis=1).astype(BF16)

    def fox_pv(p, off, width, h):
        v1 = jnp.concatenate([vb_ref[pl.ds(off, width), pair_lanes(h)], jnp.ones((width, LANES), BF16)], axis=1)
        o = jnp.dot(p, v1, preferred_element_type=F32)
        return o[:, :LANES], o[:, LANES:]

    qa = head_queries(qa_ref)
    diag_tasks = [t for h in heads for t in ((2 * i + 1, h, hi_rows, True), (2 * i, h, hi_rows, False),
                                             (2 * i, h, lo_rows, True))]
    zs = sb_qk(diag_tasks, qa)
    fz = [(fox_qk(diag_off, TQ, h, hi_rows), fox_qk(diag_off, TK, h, lo_rows)) for h in heads]
    for h in heads:
        s_hi = fox_logits(fz[h][0], diag_off, TQ, h, hi_rows)
        s_hi = jnp.concatenate([s_hi[:, :TK], jnp.where(causal, s_hi[:, TK:], NEG_BIG)], axis=1)
        s_lo = jnp.where(causal, fox_logits(fz[h][1], diag_off, TK, h, lo_rows), NEG_BIG)
        s_ref[h, hi_rows, diag_off:diag_off + TQ] = s_hi
        s_ref[h, lo_rows, diag_off:diag_off + TK] = s_lo
        m_ref[h, hi_rows, :] = chunk_max(s_hi)
        m_ref[h, lo_rows, :] = chunk_max(s_lo)
    cs, res = [], []
    for k in range(len(diag_tasks) + SB_PV_LAG):
        if k < len(diag_tasks):
            cs += sb_cumsum(diag_tasks[k:k + 1], zs[k:k + 1])
        if k >= SB_PV_LAG:
            res += sb_pv(diag_tasks[k - SB_PV_LAG:k - SB_PV_LAG + 1], zs, cs[k - SB_PV_LAG:k - SB_PV_LAG + 1])
    lowest_hi = min_mass([cs[3 * h][1] + cs[3 * h + 1][1] for h in heads])
    mass_lo, mass_hi = [], []
    for h in heads:
        (o_b, r_b), (o_a, r_a), (o_l, r_l) = res[3 * h:3 * h + 3]
        acc_a[acc_rows[h], hi_rows, :] = o_b + o_a * jnp.exp2(-r_b)
        acc_a[acc_rows[h], lo_rows, :] = o_l
        mass_lo.append(r_l)
        mass_hi.append(r_a + r_b)

    for off in range(0, diag_off, TQ):
        for h in heads:
            s = fox_logits(fox_qk(off, TQ, h, all_rows), off, TQ, h, all_rows)
            s_ref[h, :, off:off + TQ] = s
            m_ref[h] = jnp.maximum(m_ref[h], chunk_max(s))

    for h in heads:
        m_ref[h] = jnp.broadcast_to(jnp.max(m_ref[h], axis=1, keepdims=True), (TQ, LANES))
    first_tasks = sb_half_tasks(2 * i - 1, lo_rows) if i > 0 else []
    zs = sb_qk(first_tasks, qa)
    probs = [(fox_probs(diag_off, TQ, h, hi_rows), fox_probs(diag_off, TK, h, lo_rows)) for h in heads]
    for h in heads:
        acc_b[h, hi_rows, :], l_ref[h, hi_rows, :] = fox_pv(probs[h][0], diag_off, TQ, h)
        acc_b[h, lo_rows, :], l_ref[h, lo_rows, :] = fox_pv(probs[h][1], diag_off, TK, h)
    if first_tasks:
        cs = sb_cumsum(first_tasks, zs)
        lowest_lo = min_mass([mass_lo[h] + cs[h][1] for h in heads])
        mass_lo = sb_add(sb_pv(first_tasks, zs, cs), lo_rows, mass_lo)
    else:
        lowest_lo = min_mass(mass_lo)

    for off in range(0, diag_off, TQ):
        for h in heads:
            o, l = fox_pv(fox_probs(off, TQ, h, all_rows), off, TQ, h)
            acc_b[h] += o
            l_ref[h] += l

    yb = [acc_b[h] / l_ref[h] for h in heads]
    for p in range(PAIRS_PER_STEP):
        h0, h1 = p * HEADS_PER_BLOCK, p * HEADS_PER_BLOCK + 1
        yb_ref[diag_off:diag_off + TQ, p * LANES:(p + 1) * LANES] = (
            jnp.where(lane < HEAD_DIM, yb[h0], yb[h1]).astype(BF16))

    return [sb_stream(lo_rows, 2 * i - 2, mass_lo, lowest_lo), sb_stream(hi_rows, 2 * i - 1, mass_hi, lowest_hi)]


def _attn_body(*refs, nq):
    streams = [run for i in range(nq) for run in _attn_qblock(i, *refs)]
    for run in streams:
        run()
    ya_ref, acc_a = refs[9], refs[11]
    lane = lax.broadcasted_iota(jnp.int32, (TQ, LANES), 1)
    for i in range(nq):
        for p in range(PAIRS_PER_STEP):
            h0 = i * HEADS_PER_STEP + p * HEADS_PER_BLOCK
            ya_ref[i * TQ:(i + 1) * TQ, p * LANES:(p + 1) * LANES] = (
                jnp.where(lane < HEAD_DIM, acc_a[h0], acc_a[h0 + 1]).astype(BF16))


def _attention(qkv, tri_u, cf_col, cf_row, batch, seq):
    nb = N_HEAD_BLOCKS // PAIRS_PER_STEP
    step_lanes = PAIRS_PER_STEP * LANES

    def seq_spec(col):
        return pl.BlockSpec((seq, step_lanes), lambda b, g: (b, col + g))

    y_spec = seq_spec(0)
    y_shape = jax.ShapeDtypeStruct((batch * seq, WIDTH), BF16)
    head_tile = pltpu.VMEM((HEADS_PER_STEP, TQ, LANES), F32)
    nq = seq // TQ
    return pl.pallas_call(
        functools.partial(_attn_body, nq=nq),
        grid=(batch, nb),
        in_specs=[seq_spec(c * nb) for c in range(6)]
        + [_const_spec((TK, TK)),
           pl.BlockSpec((seq, LANES), lambda b, g: (b, 0)),
           pl.BlockSpec((1, N_HEADS, seq), lambda b, g: (b, 0, 0))],
        out_specs=[y_spec, y_spec],
        out_shape=[y_shape, y_shape],
        scratch_shapes=[pltpu.VMEM((nq * HEADS_PER_STEP, TQ, LANES), F32),
                        pltpu.VMEM((HEADS_PER_STEP, TQ, seq), F32), head_tile, head_tile, head_tile],
        compiler_params=pltpu.CompilerParams(dimension_semantics=("parallel", "parallel"),
                                             vmem_limit_bytes=VMEM_LIMIT),
        name="attention",
    )(qkv, qkv, qkv, qkv, qkv, qkv, tri_u, cf_col, cf_row)


def _post_body(x_ref, ya_ref, yb_ref, g_ref, wgate_ref, bgate_ref, wua_ref, wub_ref, wout_ref, o_ref):
    x = x_ref[...]
    h = _rms(x, g_ref[...]).astype(BF16)
    ya = ya_ref[...]
    yb = yb_ref[...]
    acc = jnp.zeros(x.shape, F32)
    nchunk = D_MODEL // WIDTH
    for c in range(nchunk):
        sl_a = slice(c * WIDTH, (c + 1) * WIDTH)
        sl_b = slice(D_MODEL + c * WIDTH, D_MODEL + (c + 1) * WIDTH)
        ga = jax.nn.sigmoid(jnp.dot(h, wgate_ref[:, sl_a].astype(BF16), preferred_element_type=F32)
                            + bgate_ref[:, sl_a])
        gb = jax.nn.sigmoid(jnp.dot(h, wgate_ref[:, sl_b].astype(BF16), preferred_element_type=F32)
                            + bgate_ref[:, sl_b])
        ua = jnp.dot(ya, wua_ref[:, sl_a].astype(BF16), preferred_element_type=F32)
        ub = jnp.dot(yb, wub_ref[:, sl_a].astype(BF16), preferred_element_type=F32)
        mixed = (ga * ua + gb * ub).astype(BF16)
        acc = acc + jnp.dot(mixed, wout_ref[sl_a, :].astype(BF16), preferred_element_type=F32)
    o_ref[...] = x + acc


def _post(x2d, ya, yb, g, wgate, bgate, wua, wub, wout):
    t = x2d.shape[0]
    tile = pl.BlockSpec((TM_FFN, D_MODEL), lambda i: (i, 0))
    ytile = pl.BlockSpec((TM_FFN, WIDTH), lambda i: (i, 0))
    return pl.pallas_call(
        _post_body,
        grid=(t // TM_FFN,),
        in_specs=[tile, ytile, ytile, _const_spec((1, D_MODEL)), _const_spec((D_MODEL, 2 * D_MODEL)),
                  _const_spec((1, 2 * D_MODEL)), _const_spec((WIDTH, D_MODEL)),
                  _const_spec((WIDTH, D_MODEL)), _const_spec((D_MODEL, D_MODEL))],
        out_specs=tile,
        out_shape=jax.ShapeDtypeStruct((t, D_MODEL), F32),
        compiler_params=pltpu.CompilerParams(dimension_semantics=("parallel",),
                                             vmem_limit_bytes=VMEM_LIMIT),
        name="post",
    )(x2d, ya, yb, g, wgate, bgate, wua, wub, wout)


def kernel(x, norm_ffn1, w_ffn1_gate, w_ffn1_up, w_ffn1_down, norm_mix, w_in, b_forget, w_gate, b_gate,
           w_up_a, w_up_b, w_out, norm_ffn2, w_ffn2_gate, w_ffn2_up, w_ffn2_down, norm_final):
    batch, seq, _ = x.shape
    depth = norm_ffn1.shape[0]
    x2d = x.reshape(batch * seq, D_MODEL)
    gf = norm_final.reshape(1, D_MODEL)

    row = lax.broadcasted_iota(jnp.int32, (TM_PROJ, TM_PROJ), 0)
    col = lax.broadcasted_iota(jnp.int32, (TM_PROJ, TM_PROJ), 1)
    tri_l = (row <= col).astype(BF16)
    rk = lax.broadcasted_iota(jnp.int32, (TK, TK), 0)
    ck = lax.broadcasted_iota(jnp.int32, (TK, TK), 1)
    tri_u = (rk > ck).astype(BF16)

    for l in range(depth):
        last = l == depth - 1
        x2d = _ffn(x2d, norm_ffn1[l].reshape(1, D_MODEL), w_ffn1_gate[l], w_ffn1_up[l], w_ffn1_down[l], gf,
                   final_norm=False)

        n_f = w_in.shape[2] - 6 * WIDTH
        w_in_t = jnp.swapaxes(w_in, 1, 2)
        w_f = jnp.pad(w_in_t[l, 6 * WIDTH:, :], ((0, LANES - n_f), (0, 0))).astype(BF16)
        bf_p = jnp.pad(b_forget[l], (0, LANES - n_f)).reshape(1, LANES)
        g_mix = norm_mix[l].reshape(1, D_MODEL)
        qkv, cf_col, cf_row = _proj(x2d, g_mix, w_in_t, l, w_f, bf_p, tri_l, batch, seq)
        y_a, y_b = _attention(qkv, tri_u, cf_col, cf_row, batch, seq)
        x2d = _post(x2d, y_a, y_b, g_mix, w_gate[l], b_gate[l].reshape(1, 2 * D_MODEL), w_up_a[l], w_up_b[l], w_out[l])

        x2d = _ffn(x2d, norm_ffn2[l].reshape(1, D_MODEL), w_ffn2_gate[l], w_ffn2_up[l], w_ffn2_down[l], gf,
                   final_norm=last)
    return x2d.reshape(batch, seq, D_MODEL)
```

```python
import functools

import jax
import jax.numpy as jnp
from jax import lax
from jax.experimental import pallas as pl
from jax.experimental.pallas import tpu as pltpu

D_MODEL = 1024
D_FF = 2816
HEAD_DIM = 64
N_HEADS = 8
WIDTH = N_HEADS * HEAD_DIM
RMS_EPS = 1e-6
ATTN_SCALE = HEAD_DIM ** -0.5
LOG2E = 1.4426950408889634

LANES = 128
HEADS_PER_BLOCK = LANES // HEAD_DIM
N_HEAD_BLOCKS = WIDTH // LANES
PAIRS_PER_STEP = 2
HEADS_PER_STEP = PAIRS_PER_STEP * HEADS_PER_BLOCK

TM_FFN = 1024
TF = 256
TM_PROJ = 1024
HEAD_ROWS = 16
TK = 256
TQ = 2 * TK
VMEM_LIMIT = 60 * 1024 * 1024

F32 = jnp.float32
BF16 = jnp.bfloat16
NEG_BIG = -1e30
_CONTRACT_LAST = (((1,), (1,)), ((), ()))
SB_MASS_CUTOFF = 160.0


def _rms(x, g):
    ms = jnp.mean(x * x, axis=-1, keepdims=True)
    return x * lax.rsqrt(ms + RMS_EPS) * g


def _const_spec(shape):
    nd = len(shape)
    return pl.BlockSpec(shape, lambda *_: (0,) * nd, pipeline_mode=pl.Buffered(1))


def _ffn_body(x_ref, g_ref, wg_ref, wu_ref, wd_ref, gf_ref, o_ref, *, final_norm):
    x = x_ref[...]
    xn = _rms(x, g_ref[...]).astype(BF16)
    acc = jnp.zeros(x.shape, F32)
    for c in range(D_FF // TF):
        sl = slice(c * TF, (c + 1) * TF)
        g = jnp.dot(xn, wg_ref[:, sl].astype(BF16), preferred_element_type=F32)
        u = jnp.dot(xn, wu_ref[:, sl].astype(BF16), preferred_element_type=F32)
        h = (g * jax.nn.sigmoid(g) * u).astype(BF16)
        acc = acc + jnp.dot(h, wd_ref[sl, :].astype(BF16), preferred_element_type=F32)
    y = x + 0.5 * acc
    if final_norm:
        y = _rms(y, gf_ref[...])
    o_ref[...] = y


def _ffn(x2d, g, wg, wu, wd, gf, *, final_norm):
    t = x2d.shape[0]
    tile = pl.BlockSpec((TM_FFN, D_MODEL), lambda i: (i, 0))
    return pl.pallas_call(
        functools.partial(_ffn_body, final_norm=final_norm),
        grid=(t // TM_FFN,),
        in_specs=[tile, _const_spec((1, D_MODEL)), _const_spec((D_MODEL, D_FF)),
                  _const_spec((D_MODEL, D_FF)), _const_spec((D_FF, D_MODEL)),
                  _const_spec((1, D_MODEL))],
        out_specs=tile,
        out_shape=jax.ShapeDtypeStruct((t, D_MODEL), F32),
        compiler_params=pltpu.CompilerParams(dimension_semantics=("parallel",),
                                             vmem_limit_bytes=VMEM_LIMIT),
        name="ffn_final" if final_norm else "ffn",
    )(x2d, g, wg, wu, wd, gf)


def _proj_body(x_ref, g_ref, w_ref, wf_ref, bf_ref, tri_ref, qkv_ref, cfc_ref, cfr_ref, carry_ref):
    @pl.when(pl.program_id(1) == 0)
    def _():
        carry_ref[...] = jnp.zeros_like(carry_ref)

    h = _rms(x_ref[...], g_ref[...]).astype(BF16)

    def qkv_chunk(c):
        sl = slice(c * WIDTH, (c + 1) * WIDTH)
        p = lax.dot_general(h, w_ref[0, sl, :].astype(BF16), _CONTRACT_LAST, preferred_element_type=F32)
        if c in (0, 3):
            p = p * (ATTN_SCALE * LOG2E)
        qkv_ref[:, sl] = p.astype(BF16)

    fl = lax.dot_general(h, wf_ref[...], _CONTRACT_LAST, preferred_element_type=F32) + bf_ref[...]
    lf = (jnp.minimum(fl, 0.0) - jnp.log1p(jnp.exp(-jnp.abs(fl)))) * LOG2E
    lf_t = lf.T[:HEAD_ROWS, :]
    hi = lf_t.astype(BF16)
    r1 = lf_t - hi.astype(F32)
    mid = r1.astype(BF16)
    lo = (r1 - mid.astype(F32)).astype(BF16)
    for c in range(3):
        qkv_chunk(c)
    cs = jnp.dot(jnp.concatenate([hi, mid, lo], axis=0), tri_ref[...], preferred_element_type=F32)
    for c in range(3, 6):
        qkv_chunk(c)
    cf_t = cs[:HEAD_ROWS] + cs[HEAD_ROWS:2 * HEAD_ROWS] + cs[2 * HEAD_ROWS:] + carry_ref[:, 0:1]
    carry_ref[...] = jnp.broadcast_to(cf_t[:, TM_PROJ - 1:TM_PROJ], (HEAD_ROWS, LANES))
    cfr_ref[0] = cf_t[:N_HEADS, :]
    cfc_ref[...] = jnp.concatenate([cf_t, jnp.zeros((LANES - HEAD_ROWS, TM_PROJ), F32)], axis=0).T


def _proj(x2d, g, w_t, layer, wf_t, bf, tri, batch, seq):
    t = x2d.shape[0]
    ns = seq // TM_PROJ
    return pl.pallas_call(
        _proj_body,
        grid=(batch, ns),
        in_specs=[pl.BlockSpec((TM_PROJ, D_MODEL), lambda b, s: (b * ns + s, 0)),
                  _const_spec((1, D_MODEL)),
                  pl.BlockSpec((1,) + w_t.shape[1:], lambda b, s: (layer, 0, 0), pipeline_mode=pl.Buffered(1)),
                  _const_spec((LANES, D_MODEL)),
                  _const_spec((1, LANES)), _const_spec((TM_PROJ, TM_PROJ))],
        out_specs=[pl.BlockSpec((TM_PROJ, 6 * WIDTH), lambda b, s: (b * ns + s, 0)),
                   pl.BlockSpec((TM_PROJ, LANES), lambda b, s: (b * ns + s, 0)),
                   pl.BlockSpec((1, N_HEADS, TM_PROJ), lambda b, s: (b, 0, s))],
        out_shape=[jax.ShapeDtypeStruct((t, 6 * WIDTH), BF16),
                   jax.ShapeDtypeStruct((t, LANES), F32),
                   jax.ShapeDtypeStruct((batch, N_HEADS, seq), F32)],
        scratch_shapes=[pltpu.VMEM((HEAD_ROWS, LANES), F32)],
        compiler_params=pltpu.CompilerParams(dimension_semantics=("arbitrary", "arbitrary"),
                                             vmem_limit_bytes=VMEM_LIMIT),
        name="proj",
    )(x2d, g, w_t, wf_t, bf, tri)


def _head_split(q):
    lane = lax.broadcasted_iota(jnp.int32, q.shape, 1)
    zero = jnp.zeros_like(q)
    return [jnp.where(lane < HEAD_DIM, q, zero), jnp.where(lane >= HEAD_DIM, q, zero)]


def _qk(qh, kj):
    return lax.dot_general(qh, kj, _CONTRACT_LAST, preferred_element_type=F32)


def _lane_chunks(x):
    return [x[:, c * LANES:(c + 1) * LANES] for c in range(x.shape[1] // LANES)]


def _aligned(x, m):
    return x if isinstance(x, int) else pl.multiple_of(x, m)


def _attn_qblock(i, qa_ref, ka_ref, va_ref, qb_ref, kb_ref, vb_ref, u_ref, cfc_ref, cfr_ref, ya_ref, yb_ref,
                 acc_a, s_ref, m_ref, l_ref, acc_b):
    first_head = pl.program_id(1) * HEADS_PER_STEP
    heads = range(HEADS_PER_STEP)

    def pair_lanes(h):
        p = h // HEADS_PER_BLOCK
        return slice(p * LANES, (p + 1) * LANES)

    def head_queries(q_ref):
        return [qh for p in range(PAIRS_PER_STEP)
                for qh in _head_split(q_ref[diag_off:diag_off + TQ, p * LANES:(p + 1) * LANES])]
    lo_rows, hi_rows, all_rows = slice(0, TK), slice(TK, TQ), slice(0, TQ)
    t_idx = lax.broadcasted_iota(jnp.int32, (TK, TK), 0)
    s_idx = lax.broadcasted_iota(jnp.int32, (TK, TK), 1)
    strict = s_idx < t_idx
    causal = s_idx <= t_idx
    lane = lax.broadcasted_iota(jnp.int32, (TQ, LANES), 1)
    diag_off = i * TQ
    acc_rows = range(i * HEADS_PER_STEP, (i + 1) * HEADS_PER_STEP)

    def key_off(j):
        return _aligned(j * TK, TK)

    def sb_qk(tasks, qa):
        return [_qk(qa[h][rows], ka_ref[pl.ds(key_off(j), TK), pair_lanes(h)]) for j, h, rows, _ in tasks]

    def sb_cumsum(tasks, zs):
        outs = []
        for z, (_, _, _, diag) in zip(zs, tasks):
            sp = jnp.maximum(z, 0.0) + jnp.log2(1.0 + jnp.exp2(jnp.minimum(z, -z)))
            log_beta = z - sp
            if diag:
                sp = jnp.where(strict, sp, 0.0)
            later = jnp.dot(sp.astype(BF16), u_ref[...], preferred_element_type=F32)
            outs.append((log_beta - later, later[:, 0:1] + sp[:, 0:1]))
        return outs

    def sb_pv(tasks, zs, cs):
        outs = []
        for (log_w, tot), (j, h, _, diag) in zip(cs, tasks):
            w = jnp.exp2(log_w)
            if diag:
                w = jnp.where(strict, w, 0.0)
            o = jnp.dot(w.astype(BF16), va_ref[pl.ds(key_off(j), TK), pair_lanes(h)], preferred_element_type=F32)
            outs.append((o, tot))
        return outs

    def sb_half_tasks(j, rows):
        return [(j, h, rows, False) for h in heads]

    def sb_add(res, rows, mass):
        new_mass = []
        for h in heads:
            o, t = res[h]
            acc_a[acc_rows[h], rows, :] += o * jnp.exp2(-mass[h])
            new_mass.append(mass[h] + t)
        return new_mass

    def min_mass(mass):
        return jnp.min(functools.reduce(jnp.minimum, mass))

    def sb_stream(rows, j_first, mass, lowest):
        def live(carry):
            return jnp.logical_and(carry[0] >= 0, carry[1] < SB_MASS_CUTOFF)

        def step(carry):
            j, mass = carry[0], carry[2:]
            tasks = sb_half_tasks(j, rows)
            zs = sb_qk(tasks, head_queries(qa_ref))
            new_mass = sb_add(sb_pv(tasks, zs, sb_cumsum(tasks, zs)), rows, mass)
            return (j - 1, min_mass(new_mass), *new_mass)

        return lambda: lax.while_loop(live, step, (jnp.int32(j_first), lowest, *mass))

    qb = head_queries(qb_ref)
    cfc = cfc_ref[diag_off:diag_off + TQ, :]
    fox_heads = [first_head + h for h in heads]
    cf_t = [jnp.broadcast_to(jnp.sum(jnp.where(lane == hd, cfc, 0.0), axis=1, keepdims=True), (TQ, LANES))
            for hd in fox_heads]

    def fox_qk(off, width, h, rows):
        return _qk(qb[h][rows], kb_ref[pl.ds(off, width), pair_lanes(h)])

    def fox_logits(z, off, width, h, rows):
        cf_s = cfr_ref[0, pl.ds(fox_heads[h], 1), pl.ds(off, width)]
        return jnp.concatenate([zc + cf_t[h][rows] for zc in _lane_chunks(z)], axis=1) - cf_s

    def chunk_max(s):
        return functools.reduce(jnp.maximum, _lane_chunks(s))

    def fox_probs(off, width, h, rows):
        s = s_ref[h, rows, pl.ds(off, width)]
        m = m_ref[h, rows, :]
        return jnp.concatenate([jnp.exp2(sc - m) for sc in _lane_chunks(s)], axis=1).astype(BF16)

    def fox_pv(p, off, width, h):
        v1 = jnp.concatenate([vb_ref[pl.ds(off, width), pair_lanes(h)], jnp.ones((width, LANES), BF16)], axis=1)
        o = jnp.dot(p, v1, preferred_element_type=F32)
        return o[:, :LANES], o[:, LANES:]

    qa = head_queries(qa_ref)
    diag_tasks = [t for h in heads for t in ((2 * i + 1, h, hi_rows, True), (2 * i, h, hi_rows, False),
                                             (2 * i, h, lo_rows, True))]
    zs = sb_qk(diag_tasks, qa)
    fz = [(fox_qk(diag_off, TQ, h, hi_rows), fox_qk(diag_off, TK, h, lo_rows)) for h in heads]
    cs = sb_cumsum(diag_tasks, zs)
    lowest_hi = min_mass([cs[3 * h][1] + cs[3 * h + 1][1] for h in heads])
    for h in heads:
        s_hi = fox_logits(fz[h][0], diag_off, TQ, h, hi_rows)
        s_hi = jnp.concatenate([s_hi[:, :TK], jnp.where(causal, s_hi[:, TK:], NEG_BIG)], axis=1)
        s_lo = jnp.where(causal, fox_logits(fz[h][1], diag_off, TK, h, lo_rows), NEG_BIG)
        s_ref[h, hi_rows, diag_off:diag_off + TQ] = s_hi
        s_ref[h, lo_rows, diag_off:diag_off + TK] = s_lo
        m_ref[h, hi_rows, :] = chunk_max(s_hi)
        m_ref[h, lo_rows, :] = chunk_max(s_lo)
    res = sb_pv(diag_tasks, zs, cs)
    mass_lo, mass_hi = [], []
    for h in heads:
        (o_b, r_b), (o_a, r_a), (o_l, r_l) = res[3 * h:3 * h + 3]
        acc_a[acc_rows[h], hi_rows, :] = o_b + o_a * jnp.exp2(-r_b)
        acc_a[acc_rows[h], lo_rows, :] = o_l
        mass_lo.append(r_l)
        mass_hi.append(r_a + r_b)

    for off in range(0, diag_off, TQ):
        for h in heads:
            s = fox_logits(fox_qk(off, TQ, h, all_rows), off, TQ, h, all_rows)
            s_ref[h, :, off:off + TQ] = s
            m_ref[h] = jnp.maximum(m_ref[h], chunk_max(s))

    for h in heads:
        m_ref[h] = jnp.broadcast_to(jnp.max(m_ref[h], axis=1, keepdims=True), (TQ, LANES))
    first_tasks = sb_half_tasks(2 * i - 1, lo_rows) if i > 0 else []
    zs = sb_qk(first_tasks, qa)
    probs = [(fox_probs(diag_off, TQ, h, hi_rows), fox_probs(diag_off, TK, h, lo_rows)) for h in heads]
    for h in heads:
        acc_b[h, hi_rows, :], l_ref[h, hi_rows, :] = fox_pv(probs[h][0], diag_off, TQ, h)
        acc_b[h, lo_rows, :], l_ref[h, lo_rows, :] = fox_pv(probs[h][1], diag_off, TK, h)
    if first_tasks:
        cs = sb_cumsum(first_tasks, zs)
        lowest_lo = min_mass([mass_lo[h] + cs[h][1] for h in heads])
        mass_lo = sb_add(sb_pv(first_tasks, zs, cs), lo_rows, mass_lo)
    else:
        lowest_lo = min_mass(mass_lo)

    for off in range(0, diag_off, TQ):
        for h in heads:
            o, l = fox_pv(fox_probs(off, TQ, h, all_rows), off, TQ, h)
            acc_b[h] += o
            l_ref[h] += l

    yb = [acc_b[h] / l_ref[h] for h in heads]
    for p in range(PAIRS_PER_STEP):
        h0, h1 = p * HEADS_PER_BLOCK, p * HEADS_PER_BLOCK + 1
        yb_ref[diag_off:diag_off + TQ, p * LANES:(p + 1) * LANES] = (
            jnp.where(lane < HEAD_DIM, yb[h0], yb[h1]).astype(BF16))

    return [sb_stream(lo_rows, 2 * i - 2, mass_lo, lowest_lo), sb_stream(hi_rows, 2 * i - 1, mass_hi, lowest_hi)]


def _attn_body(*refs, nq):
    streams = [run for i in range(nq) for run in _attn_qblock(i, *refs)]
    for run in streams:
        run()
    ya_ref, acc_a = refs[9], refs[11]
    lane = lax.broadcasted_iota(jnp.int32, (TQ, LANES), 1)
    for i in range(nq):
        for p in range(PAIRS_PER_STEP):
            h0 = i * HEADS_PER_STEP + p * HEADS_PER_BLOCK
            ya_ref[i * TQ:(i + 1) * TQ, p * LANES:(p + 1) * LANES] = (
                jnp.where(lane < HEAD_DIM, acc_a[h0], acc_a[h0 + 1]).astype(BF16))


def _attention(qkv, tri_u, cf_col, cf_row, batch, seq):
    nb = N_HEAD_BLOCKS // PAIRS_PER_STEP
    step_lanes = PAIRS_PER_STEP * LANES

    def seq_spec(col):
        return pl.BlockSpec((seq, step_lanes), lambda b, g: (b, col + g))

    y_spec = seq_spec(0)
    y_shape = jax.ShapeDtypeStruct((batch * seq, WIDTH), BF16)
    head_tile = pltpu.VMEM((HEADS_PER_STEP, TQ, LANES), F32)
    nq = seq // TQ
    return pl.pallas_call(
        functools.partial(_attn_body, nq=nq),
        grid=(batch, nb),
        in_specs=[seq_spec(c * nb) for c in range(6)]
        + [_const_spec((TK, TK)),
           pl.BlockSpec((seq, LANES), lambda b, g: (b, 0)),
           pl.BlockSpec((1, N_HEADS, seq), lambda b, g: (b, 0, 0))],
        out_specs=[y_spec, y_spec],
        out_shape=[y_shape, y_shape],
        scratch_shapes=[pltpu.VMEM((nq * HEADS_PER_STEP, TQ, LANES), F32),
                        pltpu.VMEM((HEADS_PER_STEP, TQ, seq), F32), head_tile, head_tile, head_tile],
        compiler_params=pltpu.CompilerParams(dimension_semantics=("parallel", "parallel"),
                                             vmem_limit_bytes=VMEM_LIMIT),
        name="attention",
    )(qkv, qkv, qkv, qkv, qkv, qkv, tri_u, cf_col, cf_row)


def _post_body(x_ref, ya_ref, yb_ref, g_ref, wgate_ref, bgate_ref, wua_ref, wub_ref, wout_ref, o_ref):
    x = x_ref[...]
    h = _rms(x, g_ref[...]).astype(BF16)
    ya = ya_ref[...]
    yb = yb_ref[...]
    acc = jnp.zeros(x.shape, F32)
    nchunk = D_MODEL // WIDTH
    for c in range(nchunk):
        sl_a = slice(c * WIDTH, (c + 1) * WIDTH)
        sl_b = slice(D_MODEL + c * WIDTH, D_MODEL + (c + 1) * WIDTH)
        ga = jax.nn.sigmoid(jnp.dot(h, wgate_ref[:, sl_a].astype(BF16), preferred_element_type=F32)
                            + bgate_ref[:, sl_a])
        gb = jax.nn.sigmoid(jnp.dot(h, wgate_ref[:, sl_b].astype(BF16), preferred_element_type=F32)
                            + bgate_ref[:, sl_b])
        ua = jnp.dot(ya, wua_ref[:, sl_a].astype(BF16), preferred_element_type=F32)
        ub = jnp.dot(yb, wub_ref[:, sl_a].astype(BF16), preferred_element_type=F32)
        mixed = (ga * ua + gb * ub).astype(BF16)
        acc = acc + jnp.dot(mixed, wout_ref[sl_a, :].astype(BF16), preferred_element_type=F32)
    o_ref[...] = x + acc


def _post(x2d, ya, yb, g, wgate, bgate, wua, wub, wout):
    t = x2d.shape[0]
    tile = pl.BlockSpec((TM_FFN, D_MODEL), lambda i: (i, 0))
    ytile = pl.BlockSpec((TM_FFN, WIDTH), lambda i: (i, 0))
    return pl.pallas_call(
        _post_body,
        grid=(t // TM_FFN,),
        in_specs=[tile, ytile, ytile, _const_spec((1, D_MODEL)), _const_spec((D_MODEL, 2 * D_MODEL)),
                  _const_spec((1, 2 * D_MODEL)), _const_spec((WIDTH, D_MODEL)),
                  _const_spec((WIDTH, D_MODEL)), _const_spec((D_MODEL, D_MODEL))],
        out_specs=tile,
        out_shape=jax.ShapeDtypeStruct((t, D_MODEL), F32),
        compiler_params=pltpu.CompilerParams(dimension_semantics=("parallel",),
                                             vmem_limit_bytes=VMEM_LIMIT),
        name="post",
    )(x2d, ya, yb, g, wgate, bgate, wua, wub, wout)


def kernel(x, norm_ffn1, w_ffn1_gate, w_ffn1_up, w_ffn1_down, norm_mix, w_in, b_forget, w_gate, b_gate,
           w_up_a, w_up_b, w_out, norm_ffn2, w_ffn2_gate, w_ffn2_up, w_ffn2_down, norm_final):
    batch, seq, _ = x.shape
    depth = norm_ffn1.shape[0]
    x2d = x.reshape(batch * seq, D_MODEL)
    gf = norm_final.reshape(1, D_MODEL)

    row = lax.broadcasted_iota(jnp.int32, (TM_PROJ, TM_PROJ), 0)
    col = lax.broadcasted_iota(jnp.int32, (TM_PROJ, TM_PROJ), 1)
    tri_l = (row <= col).astype(BF16)
    rk = lax.broadcasted_iota(jnp.int32, (TK, TK), 0)
    ck = lax.broadcasted_iota(jnp.int32, (TK, TK), 1)
    tri_u = (rk > ck).astype(BF16)

    for l in range(depth):
        last = l == depth - 1
        x2d = _ffn(x2d, norm_ffn1[l].reshape(1, D_MODEL), w_ffn1_gate[l], w_ffn1_up[l], w_ffn1_down[l], gf,
                   final_norm=False)

        n_f = w_in.shape[2] - 6 * WIDTH
        w_in_t = jnp.swapaxes(w_in, 1, 2)
        w_f = jnp.pad(w_in_t[l, 6 * WIDTH:, :], ((0, LANES - n_f), (0, 0))).astype(BF16)
        bf_p = jnp.pad(b_forget[l], (0, LANES - n_f)).reshape(1, LANES)
        g_mix = norm_mix[l].reshape(1, D_MODEL)
        qkv, cf_col, cf_row = _proj(x2d, g_mix, w_in_t, l, w_f, bf_p, tri_l, batch, seq)
        y_a, y_b = _attention(qkv, tri_u, cf_col, cf_row, batch, seq)
        x2d = _post(x2d, y_a, y_b, g_mix, w_gate[l], b_gate[l].reshape(1, 2 * D_MODEL), w_up_a[l], w_up_b[l], w_out[l])

        x2d = _ffn(x2d, norm_ffn2[l].reshape(1, D_MODEL), w_ffn2_gate[l], w_ffn2_up[l], w_ffn2_down[l], gf,
                   final_norm=last)
    return x2d.reshape(batch, seq, D_MODEL)
```
